```python
import jax, jax.numpy as jnp
from jax import lax
import numpy as np

D_MODEL = 2048
BATCH = 2
SEQ = 4096
DEPTH = 4
DEC_BATCH = 8
DEC_SEQ = 16
PAST_LEN = 4096

CHUNK = 64
N_MIXERS = 3
EPS = 1e-6

MLA_HEADS = 16
Q_LORA = 512
KV_LORA = 512
QK_NOPE = 128
QK_ROPE = 64
V_HEAD = 128
MLA_GATE = MLA_HEADS * V_HEAD
MLA_IN = Q_LORA + KV_LORA + QK_ROPE + MLA_GATE
MLA_SCALE = (QK_NOPE + QK_ROPE) ** -0.5
ROPE_THETA = 10000.0
Q_BLOCK = 128

D_CONV = D_MODEL
CONV_K = 31

SSM_INNER = 2 * D_MODEL
SSM_HEAD_DIM = 64
SSM_HEADS = SSM_INNER // SSM_HEAD_DIM
SSM_GROUPS = 8
D_STATE = 128
SSM_CONV_K = 4
SSM_XBC = SSM_INNER + 2 * SSM_GROUPS * D_STATE
SSM_IN = SSM_INNER + SSM_XBC + SSM_HEADS
SSD_BLOCK = 64
DT_MIN = 0.001
DT_MAX = 0.1

kernel_name = 'hybrid_mla_conv_ssd_stream_step'


def rms_norm(x, g):
    xf = x.astype(jnp.float32)
    y = xf * lax.rsqrt(jnp.mean(xf * xf, axis=-1, keepdims=True) + EPS)
    return y.astype(x.dtype) * g


def layer_norm(x, g, b):
    xf = x.astype(jnp.float32)
    mu = jnp.mean(xf, axis=-1, keepdims=True)
    var = jnp.mean(jnp.square(xf - mu), axis=-1, keepdims=True)
    return ((xf - mu) * lax.rsqrt(var + EPS)).astype(x.dtype) * g + b


def rope(x, pos):
    half = x.shape[-1] // 2
    freqs = ROPE_THETA ** (-jnp.arange(half, dtype=jnp.float32) / half)
    ang = pos.astype(jnp.float32)[:, None] * freqs[None, :]
    cos, sin = jnp.cos(ang)[:, None, :], jnp.sin(ang)[:, None, :]
    xf = x.astype(jnp.float32)
    x1, x2 = xf[..., :half], xf[..., half:]
    return jnp.concatenate([x1 * cos - x2 * sin, x1 * sin + x2 * cos], axis=-1).astype(x.dtype)


def causal_depthwise(xp, w, b):
    c = xp.shape[-1]
    y = lax.conv_general_dilated(xp, w[:, None, :], window_strides=(1,), padding='VALID',
                                 dimension_numbers=('NWC', 'WIO', 'NWC'), feature_group_count=c)
    return y + b


def mla_project(h, pos, w_in, q_norm, w_qb, kv_norm):
    q_a, c_kv, k_pe, gate = jnp.split(h @ w_in, [Q_LORA, Q_LORA + KV_LORA, Q_LORA + KV_LORA + QK_ROPE], axis=-1)
    q = jnp.einsum('blr,rhd->blhd', rms_norm(q_a, q_norm), w_qb)
    q_nope, q_pe = q[..., :QK_NOPE], rope(q[..., QK_NOPE:], pos)
    ckv = rms_norm(c_kv, kv_norm)
    kpe = rope(k_pe[:, :, None, :], pos)[:, :, 0]
    return q_nope, q_pe, ckv, kpe, gate


def mla_prompt(h, w_in, q_norm, w_qb, kv_norm, w_kvb, w_out):
    b, s, _ = h.shape
    q_nope, q_pe, ckv, kpe, gate = mla_project(h, jnp.arange(s), w_in, q_norm, w_qb, kv_norm)
    kv = jnp.einsum('bsc,chd->bshd', ckv, w_kvb)
    k = jnp.concatenate([kv[..., :QK_NOPE], jnp.broadcast_to(kpe[:, :, None, :], (b, s, MLA_HEADS, QK_ROPE))], axis=-1)
    v = kv[..., QK_NOPE:]
    q = jnp.concatenate([q_nope, q_pe], axis=-1)
    nb = s // Q_BLOCK
    q_blocks = jnp.moveaxis(q.reshape(b, nb, Q_BLOCK, MLA_HEADS, QK_NOPE + QK_ROPE), 1, 0)
    key_chunk = jnp.arange(s) // CHUNK

    def attend_block(args):
        qb, bi = args
        q_chunk = (bi * Q_BLOCK + jnp.arange(Q_BLOCK)) // CHUNK
        mask = key_chunk[None, :] <= q_chunk[:, None]
        sc = jnp.einsum('bqhd,bkhd->bhqk', qb, k).astype(jnp.float32) * MLA_SCALE
        p = jax.nn.softmax(jnp.where(mask, sc, -jnp.inf), axis=-1).astype(v.dtype)
        return jnp.einsum('bhqk,bkhd->bqhd', p, v)

    o = lax.map(attend_block, (q_blocks, jnp.arange(nb)))
    o = jnp.moveaxis(o, 0, 1).reshape(b, s, MLA_GATE)
    return (o * jax.nn.silu(gate)) @ w_out, ckv, kpe


def mla_sample(h, ckv_past, kpe_past, w_in, q_norm, w_qb, kv_norm, w_kvb, w_out):
    b, l, _ = h.shape
    pos = ckv_past.shape[1] + jnp.arange(l)
    q_nope, q_pe, ckv, kpe, gate = mla_project(h, pos, w_in, q_norm, w_qb, kv_norm)
    ckv_all = jnp.concatenate([ckv_past, ckv], axis=1)
    kpe_all = jnp.concatenate([kpe_past, kpe], axis=1)
    w_uk, w_uv = w_kvb[..., :QK_NOPE], w_kvb[..., QK_NOPE:]
    q_lat = jnp.einsum('blhd,chd->blhc', q_nope, w_uk)
    sc = (jnp.einsum('blhc,btc->bhlt', q_lat, ckv_all)
          + jnp.einsum('blhr,btr->bhlt', q_pe, kpe_all)).astype(jnp.float32) * MLA_SCALE
    p = jax.nn.softmax(sc, axis=-1).astype(ckv_all.dtype)
    o_lat = jnp.einsum('bhlt,btc->blhc', p, ckv_all)
    o = jnp.einsum('blhc,chd->blhd', o_lat, w_uv).reshape(b, l, MLA_GATE)
    return (o * jax.nn.silu(gate)) @ w_out, ckv, kpe


def conformer_conv(h, hist, w_in, dw_w, dw_b, ln_g, ln_b, w_out):
    val, glu_gate, gate = jnp.split(h @ w_in, [D_CONV, 2 * D_CONV], axis=-1)
    u = val * jax.nn.sigmoid(glu_gate)
    up = jnp.concatenate([hist, u], axis=1)
    v = jax.nn.silu(layer_norm(causal_depthwise(up, dw_w, dw_b), ln_g, ln_b))
    return (v * jax.nn.silu(gate)) @ w_out, up[:, -(CONV_K - 1):]


def ssd_scan(x, dt, a, bm, cm, h0, block):
    b, l, g, e, p = x.shape
    n = bm.shape[-1]
    nc = l // block
    dtx = (x * dt[..., None]).reshape(b, nc, block, g, e, p)
    a_cs = jnp.cumsum((dt * a).reshape(b, nc, block, g, e), axis=2)
    bm = bm.reshape(b, nc, block, g, n)
    cm = cm.reshape(b, nc, block, g, n)
    acs_t = jnp.moveaxis(a_cs, 2, -1)
    causal = jnp.tril(jnp.ones((block, block), dtype=bool))
    seg = jnp.exp(jnp.where(causal, acs_t[..., :, None] - acs_t[..., None, :], -jnp.inf))
    cb = jnp.einsum('bclgn,bcsgn->bcgls', cm, bm)
    y_diag = jnp.einsum('bcgls,bcgels,bcsgep->bclgep', cb, seg, dtx)
    decay_to_end = jnp.exp(a_cs[:, :, -1:] - a_cs)
    chunk_states = jnp.einsum('bclgn,bclge,bclgep->bcgepn', bm, decay_to_end, dtx)
    chunk_decay = jnp.exp(a_cs[:, :, -1])

    def step(state, inp):
        st, dec = inp
        return state * dec[..., None, None] + st, state

    h_final, h_in = lax.scan(step, h0, (jnp.moveaxis(chunk_states, 1, 0), jnp.moveaxis(chunk_decay, 1, 0)))
    h_in = jnp.moveaxis(h_in, 0, 1)
    y_off = jnp.einsum('bclgn,bcgepn,bclge->bclgep', cm, h_in, jnp.exp(a_cs))
    return (y_diag + y_off).reshape(b, l, g, e, p), h_final


def mamba2_ssd(h, conv_hist, ssm_state, block, w_in, conv_w, conv_b, dt_bias, a_log, d_skip, gnorm, w_out):
    b, l, _ = h.shape
    e = SSM_HEADS // SSM_GROUPS
    z, xbc, dt = jnp.split(h @ w_in, [SSM_INNER, SSM_INNER + SSM_XBC], axis=-1)
    xp = jnp.concatenate([conv_hist, xbc], axis=1)
    xbc = jax.nn.silu(causal_depthwise(xp, conv_w, conv_b))
    x, bm, cm = jnp.split(xbc.astype(jnp.float32), [SSM_INNER, SSM_INNER + SSM_GROUPS * D_STATE], axis=-1)
    x = x.reshape(b, l, SSM_GROUPS, e, SSM_HEAD_DIM)
    bm = bm.reshape(b, l, SSM_GROUPS, D_STATE)
    cm = cm.reshape(b, l, SSM_GROUPS, D_STATE)
    dt = jax.nn.softplus(dt.astype(jnp.float32) + dt_bias.astype(jnp.float32)).reshape(b, l, SSM_GROUPS, e)
    a = -jnp.exp(a_log.astype(jnp.float32)).reshape(SSM_GROUPS, e)
    h0 = ssm_state.astype(jnp.float32).reshape(b, SSM_GROUPS, e, SSM_HEAD_DIM, D_STATE)
    y, h_final = ssd_scan(x, dt, a, bm, cm, h0, block)
    y = y + d_skip.astype(jnp.float32).reshape(SSM_GROUPS, e, 1) * x
    yz = (y.reshape(b, l, SSM_INNER) * jax.nn.silu(z.astype(jnp.float32))).reshape(b, l, SSM_GROUPS, -1)
    yz = yz * lax.rsqrt(jnp.mean(yz * yz, axis=-1, keepdims=True) + EPS)
    yn = yz.reshape(b, l, SSM_INNER).astype(h.dtype) * gnorm
    new_state = h_final.reshape(b, SSM_HEADS, SSM_HEAD_DIM, D_STATE).astype(h.dtype)
    return yn @ w_out, xp[:, -(SSM_CONV_K - 1):], new_state


def setup_inputs(seed: int = 0) -> dict:
    key = jax.random.key(seed)
    ks = iter(jax.random.split(key, 64))

    def nrm(shape, scale):
        return jax.random.normal(next(ks), shape, jnp.float32) * scale

    def gain(n):
        return 1.0 + 0.02 * jax.random.normal(next(ks), (n,), jnp.float32)

    def mla_params(pre):
        return {
            pre + 'norm': gain(D_MODEL),
            pre + 'w_in': nrm((D_MODEL, MLA_IN), D_MODEL ** -0.5),
            pre + 'q_norm': gain(Q_LORA),
            pre + 'w_qb': nrm((Q_LORA, MLA_HEADS, QK_NOPE + QK_ROPE), Q_LORA ** -0.5),
            pre + 'kv_norm': gain(KV_LORA),
            pre + 'w_kvb': nrm((KV_LORA, MLA_HEADS, QK_NOPE + V_HEAD), KV_LORA ** -0.5),
            pre + 'w_out': nrm((MLA_GATE, D_MODEL), MLA_GATE ** -0.5),
        }

    inp = {}
    inp['x_prompt'] = nrm((BATCH, SEQ, D_MODEL), 1.0)
    inp['x_sample'] = nrm((DEC_BATCH, DEC_SEQ, D_MODEL), 1.0)
    inp['cache_l0_ckv'] = nrm((DEC_BATCH, PAST_LEN, KV_LORA), 1.0)
    inp['cache_l0_kpe'] = nrm((DEC_BATCH, PAST_LEN, QK_ROPE), 1.0)
    inp['state_l1_conv'] = nrm((DEC_BATCH, CONV_K - 1, D_CONV), 0.5)
    inp['state_l2_conv'] = nrm((DEC_BATCH, SSM_CONV_K - 1, SSM_XBC), 1.0)
    inp['state_l2_ssm'] = nrm((DEC_BATCH, SSM_HEADS, SSM_HEAD_DIM, D_STATE), 0.1)
    inp['cache_l3_ckv'] = nrm((DEC_BATCH, PAST_LEN, KV_LORA), 1.0)
    inp['cache_l3_kpe'] = nrm((DEC_BATCH, PAST_LEN, QK_ROPE), 1.0)
    inp.update(mla_params('l0_'))
    inp['l1_norm'] = gain(D_MODEL)
    inp['l1_w_in'] = nrm((D_MODEL, 3 * D_CONV), D_MODEL ** -0.5)
    inp['l1_dw_w'] = nrm((CONV_K, D_CONV), CONV_K ** -0.5)
    inp['l1_dw_b'] = nrm((D_CONV,), 0.02)
    inp['l1_ln_g'] = gain(D_CONV)
    inp['l1_ln_b'] = nrm((D_CONV,), 0.02)
    inp['l1_w_out'] = nrm((D_CONV, D_MODEL), D_CONV ** -0.5)
    inp['l2_norm'] = gain(D_MODEL)
    inp['l2_w_in'] = nrm((D_MODEL, SSM_IN), D_MODEL ** -0.5)
    inp['l2_conv_w'] = nrm((SSM_CONV_K, SSM_XBC), SSM_CONV_K ** -0.5)
    inp['l2_conv_b'] = nrm((SSM_XBC,), 0.02)
    u = jax.random.uniform(next(ks), (SSM_HEADS,), jnp.float32)
    dt0 = jnp.exp(u * (np.log(DT_MAX) - np.log(DT_MIN)) + np.log(DT_MIN))
    inp['l2_dt_bias'] = dt0 + jnp.log(-jnp.expm1(-dt0))
    inp['l2_a_log'] = jnp.log(jax.random.uniform(next(ks), (SSM_HEADS,), jnp.float32, 1.0, 16.0))
    inp['l2_d_skip'] = 1.0 + 0.1 * jax.random.normal(next(ks), (SSM_HEADS,), jnp.float32)
    inp['l2_gnorm'] = gain(SSM_INNER)
    inp['l2_w_out'] = nrm((SSM_INNER, D_MODEL), SSM_INNER ** -0.5)
    inp.update(mla_params('l3_'))
    inp['final_norm'] = gain(D_MODEL)
    return inp


def reference(x_prompt, x_sample, cache_l0_ckv, cache_l0_kpe, state_l1_conv, state_l2_conv, state_l2_ssm,
              cache_l3_ckv, cache_l3_kpe,
              l0_norm, l0_w_in, l0_q_norm, l0_w_qb, l0_kv_norm, l0_w_kvb, l0_w_out,
              l1_norm, l1_w_in, l1_dw_w, l1_dw_b, l1_ln_g, l1_ln_b, l1_w_out,
              l2_norm, l2_w_in, l2_conv_w, l2_conv_b, l2_dt_bias, l2_a_log, l2_d_skip, l2_gnorm, l2_w_out,
              l3_norm, l3_w_in, l3_q_norm, l3_w_qb, l3_kv_norm, l3_w_kvb, l3_w_out,
              final_norm):
    layer_norms = [l0_norm, l1_norm, l2_norm, l3_norm]
    layer_weights = [
        (l0_w_in, l0_q_norm, l0_w_qb, l0_kv_norm, l0_w_kvb, l0_w_out),
        (l1_w_in, l1_dw_w, l1_dw_b, l1_ln_g, l1_ln_b, l1_w_out),
        (l2_w_in, l2_conv_w, l2_conv_b, l2_dt_bias, l2_a_log, l2_d_skip, l2_gnorm, l2_w_out),
        (l3_w_in, l3_q_norm, l3_w_qb, l3_kv_norm, l3_w_kvb, l3_w_out),
    ]
    layer_states = [
        (cache_l0_ckv, cache_l0_kpe),
        (state_l1_conv,),
        (state_l2_conv, state_l2_ssm),
        (cache_l3_ckv, cache_l3_kpe),
    ]
    xp, xs = x_prompt, x_sample
    new_prompt, new_sample = [], []
    for i in range(DEPTH):
        kind = i % N_MIXERS
        hp = rms_norm(xp, layer_norms[i])
        hs = rms_norm(xs, layer_norms[i])
        w = layer_weights[i]
        st = layer_states[i]
        if kind == 0:
            yp, *newp = mla_prompt(hp, *w)
            ys, *news = mla_sample(hs, st[0], st[1], *w)
        elif kind == 1:
            zh = jnp.zeros((hp.shape[0], CONV_K - 1, D_CONV), hp.dtype)
            yp, *newp = conformer_conv(hp, zh, *w)
            ys, *news = conformer_conv(hs, st[0], *w)
        else:
            zc = jnp.zeros((hp.shape[0], SSM_CONV_K - 1, SSM_XBC), hp.dtype)
            zs = jnp.zeros((hp.shape[0], SSM_HEADS, SSM_HEAD_DIM, D_STATE), hp.dtype)
            yp, *newp = mamba2_ssd(hp, zc, zs, SSD_BLOCK, *w)
            ys, *news = mamba2_ssd(hs, st[0], st[1], hs.shape[1], *w)
        xp = xp + yp
        xs = xs + ys
        new_prompt.append(newp)
        new_sample.append(news)
    y_prompt = rms_norm(xp, final_norm)
    y_sample = rms_norm(xs, final_norm)
    (l0_ckv_p, l0_kpe_p), (l1_conv_p,), (l2_conv_p, l2_ssm_p), (l3_ckv_p, l3_kpe_p) = new_prompt
    (l0_ckv_s, l0_kpe_s), (l1_conv_s,), (l2_conv_s, l2_ssm_s), (l3_ckv_s, l3_kpe_s) = new_sample
    return (y_prompt, y_sample,
            l0_ckv_p, l0_kpe_p, l0_ckv_s, l0_kpe_s,
            l1_conv_p, l1_conv_s,
            l2_conv_p, l2_ssm_p, l2_conv_s, l2_ssm_s,
            l3_ckv_p, l3_kpe_p, l3_ckv_s, l3_kpe_s)
```

```python
import functools

import jax
import jax.numpy as jnp
from jax import lax
from jax.experimental import pallas as pl
from jax.experimental.pallas import tpu as pltpu

F32 = jnp.float32
BF16 = jnp.bfloat16

EPS = 1e-6
ROPE_THETA = 10000.0
CHUNK = 64
HEADS = 16
NOPE = 128
ROPE = 64
HEAD_PAD = 256
MLA_SCALE = (NOPE + ROPE) ** -0.5
LORA = 512
CONV_K = 31
CONV_HIST = 32
SSM_GROUPS = 8
SSM_GROUP_HEADS = 8
SSM_HEAD_DIM = 64
SSM_GROUP_W = SSM_GROUP_HEADS * SSM_HEAD_DIM
D_STATE = 128
SSM_CONV_K = 4
SSM_HIST = 8
LANE = 128

VMEM_LIMIT_BYTES = 56 * 1024 * 1024


def _cparams(*sem):
    return pltpu.CompilerParams(dimension_semantics=sem, vmem_limit_bytes=VMEM_LIMIT_BYTES)


def _resident(shape, index_map):
    return pl.BlockSpec(shape, index_map, pipeline_mode=pl.Buffered(1))


def _sigmoid(x):
    return 1.0 / (1.0 + jnp.exp(-x))


def _silu(x):
    return x * _sigmoid(x)


def _rms(x, g):
    return (x * lax.rsqrt(jnp.mean(x * x, axis=-1, keepdims=True) + EPS)) * g


def _dot(a, b):
    return jnp.dot(a, b, preferred_element_type=F32)


def _dot_nt(a, b):
    return lax.dot_general(a, b, (((1,), (1,)), ((), ())), preferred_element_type=F32)


def _norm_matmul_kernel(x_ref, g_ref, w_ref, o_ref, *, col_chunk):
    xn = _rms(x_ref[...], g_ref[...]).astype(BF16)
    for c in range(0, o_ref.shape[1], col_chunk):
        o_ref[:, c:c + col_chunk] = _dot(xn, w_ref[:, c:c + col_chunk]).astype(o_ref.dtype)


def norm_matmul(x, g, w, *, tm, tn, out_dtype, name):
    m, k = x.shape
    n = w.shape[1]
    col_chunk = 256 if tn % 256 == 0 else LANE
    return pl.pallas_call(
        functools.partial(_norm_matmul_kernel, col_chunk=col_chunk),
        grid=(n // tn, m // tm),
        in_specs=[pl.BlockSpec((tm, k), lambda j, i: (i, 0)),
                  _resident((1, k), lambda j, i: (0, 0)),
                  _resident((k, tn), lambda j, i: (0, j))],
        out_specs=pl.BlockSpec((tm, tn), lambda j, i: (i, j)),
        out_shape=jax.ShapeDtypeStruct((m, n), out_dtype),
        compiler_params=_cparams("arbitrary", "arbitrary"),
        name=name,
    )(x, g.reshape(1, k), w)


def _out_proj_kernel(*refs, gated, final):
    refs = list(refs)
    a_ref = refs.pop(0)
    gate_ref = refs.pop(0) if gated else None
    w_ref, res_ref = refs.pop(0), refs.pop(0)
    fin_ref = refs.pop(0) if final else None
    o_ref = refs.pop(0)
    a = a_ref[...]
    if gated:
        a = (a.astype(F32) * _silu(gate_ref[...].astype(F32))).astype(BF16)
    y = res_ref[...] + _dot(a, w_ref[...])
    if final:
        y = _rms(y, fin_ref[...])
    o_ref[...] = y


def out_proj(a, gate_src, w, res, fin, *, tm, name):
    m, k = a.shape
    n = w.shape[1]
    gated, final = gate_src is not None, fin is not None
    in_specs = [pl.BlockSpec((tm, k), lambda i: (i, 0))]
    args = [a]
    if gated:
        in_specs.append(pl.BlockSpec((tm, k), lambda i: (i, 0)))
        args.append(gate_src)
    in_specs += [_resident((k, n), lambda i: (0, 0)), pl.BlockSpec((tm, n), lambda i: (i, 0))]
    args += [w, res]
    if final:
        in_specs.append(_resident((1, n), lambda i: (0, 0)))
        args.append(fin.reshape(1, n))
    return pl.pallas_call(
        functools.partial(_out_proj_kernel, gated=gated, final=final),
        grid=(m // tm,),
        in_specs=in_specs,
        out_specs=pl.BlockSpec((tm, n), lambda i: (i, 0)),
        out_shape=jax.ShapeDtypeStruct((m, n), F32),
        compiler_params=_cparams("parallel"),
        name=name,
    )(*args)


MLA_GATE_W = HEADS * NOPE
MLA_PROJ_W = MLA_GATE_W + 2 * LORA + 2 * LANE


def _mla_mid_kernel(*refs, with_kv):
    (qa_ref, ckv_ref, kpe_ref, cos_ref, sin_ref, qn_ref, kvn_ref,
     wqn_ref, wqr_ref, wqr2_ref) = refs[:10]
    refs = refs[10:]
    if with_kv:
        wk_ref, wv_ref, q_out, ckv_out, kpe_out, k_out, v_out = refs
    else:
        q_out, ckv_out, kpe_out = refs
    cosp, sinp = cos_ref[...], sin_ref[...]

    qa = _rms(qa_ref[...].astype(F32), qn_ref[...]).astype(BF16)
    qn = _dot(qa, wqn_ref[...]) * MLA_SCALE
    qr = _dot(qa, wqr_ref[...])
    qr2 = _dot(qa, wqr2_ref[...])
    for h in range(HEADS):
        lo = h * LANE
        q_out[:, 2 * lo:2 * lo + LANE] = qn[:, lo:lo + LANE].astype(BF16)
        rot = (qr[:, lo:lo + LANE] * cosp + qr2[:, lo:lo + LANE] * sinp) * MLA_SCALE
        q_out[:, 2 * lo + LANE:2 * lo + 2 * LANE] = rot.astype(BF16)

    ckv = _rms(ckv_ref[...].astype(F32), kvn_ref[...])
    ckv_out[...] = ckv
    kp = kpe_ref[...].astype(F32)
    kpe = kp[:, :LANE] * cosp + kp[:, LANE:] * sinp
    kpe_out[...] = kpe[:, :ROPE]
    if with_kv:
        ckv_b = ckv.astype(BF16)
        kpe_b = kpe.astype(BF16)
        kn = _dot(ckv_b, wk_ref[...])
        for h in range(HEADS):
            lo = h * LANE
            k_out[:, 2 * lo:2 * lo + LANE] = kn[:, lo:lo + LANE].astype(BF16)
            k_out[:, 2 * lo + LANE:2 * lo + 2 * LANE] = kpe_b
        v_out[...] = _dot(ckv_b, wv_ref[...]).astype(BF16)


def mla_mid(proj, cos_t, sin_t, q_norm, kv_norm, wqn, wqr, wqr2, wk, wv, *, tm, with_kv, name):
    m = proj.shape[0]
    nt = cos_t.shape[0] // tm
    hw = HEADS * LANE
    in_specs = [
        pl.BlockSpec((tm, LORA), lambda i: (i, MLA_GATE_W // LORA)),
        pl.BlockSpec((tm, LORA), lambda i: (i, MLA_GATE_W // LORA + 1)),
        pl.BlockSpec((tm, 2 * LANE), lambda i: (i, (MLA_GATE_W + 2 * LORA) // (2 * LANE))),
        pl.BlockSpec((tm, LANE), lambda i: (i % nt, 0)),
        pl.BlockSpec((tm, LANE), lambda i: (i % nt, 0)),
        _resident((1, LORA), lambda i: (0, 0)),
        _resident((1, LORA), lambda i: (0, 0)),
        _resident((LORA, hw), lambda i: (0, 0)),
        _resident((LORA, hw), lambda i: (0, 0)),
        _resident((LORA, hw), lambda i: (0, 0)),
    ]
    args = [proj, proj, proj, cos_t, sin_t, q_norm.reshape(1, LORA), kv_norm.reshape(1, LORA), wqn, wqr, wqr2]
    out_specs = [pl.BlockSpec((tm, HEADS * HEAD_PAD), lambda i: (i, 0)),
                 pl.BlockSpec((tm, LORA), lambda i: (i, 0)),
                 pl.BlockSpec((tm, ROPE), lambda i: (i, 0))]
    out_shape = [jax.ShapeDtypeStruct((m, HEADS * HEAD_PAD), BF16),
                 jax.ShapeDtypeStruct((m, LORA), F32),
                 jax.ShapeDtypeStruct((m, ROPE), F32)]
    if with_kv:
        in_specs += [_resident((LORA, hw), lambda i: (0, 0)), _resident((LORA, hw), lambda i: (0, 0))]
        args += [wk, wv]
        out_specs += [pl.BlockSpec((tm, HEADS * HEAD_PAD), lambda i: (i, 0)),
                      pl.BlockSpec((tm, hw), lambda i: (i, 0))]
        out_shape += [jax.ShapeDtypeStruct((m, HEADS * HEAD_PAD), BF16),
                      jax.ShapeDtypeStruct((m, hw), BF16)]
    return pl.pallas_call(
        functools.partial(_mla_mid_kernel, with_kv=with_kv),
        grid=(m // tm,),
        in_specs=in_specs, out_specs=out_specs, out_shape=out_shape,
        compiler_params=_cparams("parallel"),
        name=name,
    )(*args)


NEG_BIG = -1e30


def _flash_kernel(q_ref, k_ref, v_ref, o_ref, *, tile):
    qi = pl.program_id(2)
    q = q_ref[...]

    def step(ki, carry, masked):
        m_prev, l_prev, acc = carry
        start = pl.multiple_of(ki * tile, tile)
        s = _dot_nt(q, k_ref[pl.ds(start, tile), :])
        if masked:
            r = lax.broadcasted_iota(jnp.int32, s.shape, 0) // CHUNK
            c = lax.broadcasted_iota(jnp.int32, s.shape, 1) // CHUNK
            s = jnp.where(c <= r, s, NEG_BIG)
        m_new = jnp.maximum(m_prev, jnp.max(s, axis=-1, keepdims=True))
        alpha = jnp.exp(m_prev - m_new)
        p = jnp.exp(s - m_new)
        l_new = alpha * l_prev + jnp.sum(p, axis=-1, keepdims=True)
        acc = alpha * acc + _dot(p.astype(BF16), v_ref[pl.ds(start, tile), :])
        return m_new, l_new, acc

    init = (jnp.full((tile, 1), NEG_BIG, F32), jnp.zeros((tile, 1), F32), jnp.zeros((tile, NOPE), F32))
    carry = lax.fori_loop(0, qi, lambda ki, c: step(ki, c, False), init)
    _, l_fin, acc = step(qi, carry, True)
    o_ref[...] = (acc / l_fin).astype(o_ref.dtype)


def flash_attention(q, k, v, *, batch, seq, tile, name):
    nq = seq // tile
    return pl.pallas_call(
        functools.partial(_flash_kernel, tile=tile),
        grid=(batch, HEADS, nq),
        in_specs=[pl.BlockSpec((tile, HEAD_PAD), lambda b, h, i: (b * nq + i, h)),
                  pl.BlockSpec((seq, HEAD_PAD), lambda b, h, i: (b, h)),
                  pl.BlockSpec((seq, NOPE), lambda b, h, i: (b, h))],
        out_specs=pl.BlockSpec((tile, NOPE), lambda b, h, i: (b * nq + i, h)),
        out_shape=jax.ShapeDtypeStruct((batch * seq, HEADS * NOPE), BF16),
        compiler_params=_cparams("parallel", "parallel", "arbitrary"),
        name=name,
    )(q, k, v)


def _sample_attn_kernel(q_ref, cnew_ref, pnew_ref, ccache_ref, pcache_ref, wk_ref, wv_ref, o_ref,
                        kc_scr, kp_scr, *, past, new, pad):
    total = past + pad
    kc_scr[0:past, :] = ccache_ref[0].astype(BF16)
    kp_scr[0:past, :] = pcache_ref[0].astype(BF16)
    kc_scr[past:total, :] = jnp.zeros((pad, LORA), BF16)
    kp_scr[past:total, :] = jnp.zeros((pad, ROPE), BF16)
    kc_scr[past:past + new, :] = cnew_ref[...].astype(BF16)
    kp_scr[past:past + new, :] = pnew_ref[...].astype(BF16)

    q = q_ref[...]
    qlat, qpe = [], []
    for h in range(HEADS):
        lo = h * HEAD_PAD
        qlat.append(_dot_nt(q[:, lo:lo + NOPE], wk_ref[:, h * NOPE:(h + 1) * NOPE]))
        qpe.append(q[:, lo + NOPE:lo + NOPE + ROPE])
    qlat = jnp.concatenate(qlat, axis=0).astype(BF16)
    qpe = jnp.concatenate(qpe, axis=0)
    kc = kc_scr[...]
    s = _dot_nt(qlat, kc) + _dot_nt(qpe, kp_scr[...])
    col = lax.broadcasted_iota(jnp.int32, s.shape, 1)
    s = jnp.where(col < past + new, s, NEG_BIG)
    m = jnp.max(s, axis=-1, keepdims=True)
    p = jnp.exp(s - m)
    l = jnp.sum(p, axis=-1, keepdims=True)
    olat = (_dot(p.astype(BF16), kc) / l).astype(BF16)
    for h in range(HEADS):
        o_ref[:, h * NOPE:(h + 1) * NOPE] = _dot(
            olat[h * new:(h + 1) * new, :], wv_ref[:, h * NOPE:(h + 1) * NOPE]).astype(o_ref.dtype)


def sample_attention(q, ckv_new, kpe_new, ckv_cache, kpe_cache, wk, wv, *, name):
    nb, past, _ = ckv_cache.shape
    new = q.shape[0] // nb
    pad = LANE
    hw = HEADS * NOPE
    return pl.pallas_call(
        functools.partial(_sample_attn_kernel, past=past, new=new, pad=pad),
        grid=(nb,),
        in_specs=[pl.BlockSpec((new, HEADS * HEAD_PAD), lambda b: (b, 0)),
                  pl.BlockSpec((new, LORA), lambda b: (b, 0)),
                  pl.BlockSpec((new, ROPE), lambda b: (b, 0)),
                  pl.BlockSpec((1, past, LORA), lambda b: (b, 0, 0)),
                  pl.BlockSpec((1, past, ROPE), lambda b: (b, 0, 0)),
                  _resident((LORA, hw), lambda b: (0, 0)),
                  _resident((LORA, hw), lambda b: (0, 0))],
        out_specs=pl.BlockSpec((new, hw), lambda b: (b, 0)),
        out_shape=jax.ShapeDtypeStruct((nb * new, hw), BF16),
        scratch_shapes=[pltpu.VMEM((past + pad, LORA), BF16), pltpu.VMEM((past + pad, ROPE), BF16)],
        compiler_params=_cparams("arbitrary"),
        name=name,
    )(q, ckv_new, kpe_new, ckv_cache, kpe_cache, wk, wv)


def _conv_kernel(val_ref, glu_ref, gate_ref, hist_ref, dw_ref, dwb_ref, lng_ref, lnb_ref,
                 a_ref, tail_ref, ubuf, *, tt):
    i = pl.program_id(1)

    @pl.when(i == 0)
    def _():
        ubuf[0:CONV_HIST, :] = hist_ref[0]

    u = val_ref[...].astype(F32) * _sigmoid(glu_ref[...].astype(F32))
    ubuf[CONV_HIST:CONV_HIST + tt, :] = u
    off = CONV_HIST - (CONV_K - 1)
    acc = ubuf[off:off + tt, :] * dw_ref[0:1, :]
    for k in range(1, CONV_K):
        acc = acc + ubuf[off + k:off + k + tt, :] * dw_ref[k:k + 1, :]
    acc = acc + dwb_ref[...]
    mu = jnp.mean(acc, axis=-1, keepdims=True)
    cen = acc - mu
    var = jnp.mean(cen * cen, axis=-1, keepdims=True)
    v = _silu((cen * lax.rsqrt(var + EPS)) * lng_ref[...] + lnb_ref[...])
    a_ref[...] = (v * _silu(gate_ref[...].astype(F32))).astype(a_ref.dtype)
    tail = ubuf[tt:tt + CONV_HIST, :]
    tail_ref[0] = tail
    ubuf[0:CONV_HIST, :] = tail


def conv_mixer(proj, hist, dw_w, dw_b, ln_g, ln_b, *, batch, seq, tt, name):
    d = dw_w.shape[1]
    nt = seq // tt
    vec = lambda v: v.reshape(1, d)
    return pl.pallas_call(
        functools.partial(_conv_kernel, tt=tt),
        grid=(batch, nt),
        in_specs=[pl.BlockSpec((tt, d), lambda b, i: (b * nt + i, 0)),
                  pl.BlockSpec((tt, d), lambda b, i: (b * nt + i, 1)),
                  pl.BlockSpec((tt, d), lambda b, i: (b * nt + i, 2)),
                  pl.BlockSpec((1, CONV_HIST, d), lambda b, i: (b, 0, 0)),
                  _resident((CONV_K, d), lambda b, i: (0, 0)),
                  _resident((1, d), lambda b, i: (0, 0)),
                  _resident((1, d), lambda b, i: (0, 0)),
                  _resident((1, d), lambda b, i: (0, 0))],
        out_specs=[pl.BlockSpec((tt, d), lambda b, i: (b * nt + i, 0)),
                   pl.BlockSpec((1, CONV_HIST, d), lambda b, i: (b, 0, 0))],
        out_shape=[jax.ShapeDtypeStruct((batch * seq, d), BF16),
                   jax.ShapeDtypeStruct((batch, CONV_HIST, d), F32)],
        scratch_shapes=[pltpu.VMEM((CONV_HIST + tt, d), F32)],
        compiler_params=_cparams("parallel", "arbitrary"),
        name=name,
    )(proj, proj, proj, hist, dw_w, vec(dw_b), vec(ln_g), vec(ln_b))


def _softplus(x):
    return jnp.maximum(x, 0.0) + jnp.log1p(jnp.exp(-jnp.abs(x)))


def _ssd_kernel(z_ref, x_ref, b_ref, c_ref, dt_ref, hx_ref, hb_ref, hc_ref, st0_ref,
                cwx_ref, cwb_ref, cwc_ref, cbx_ref, cbb_ref, cbc_ref,
                dtb_ref, alog_ref, dskip_ref, gn_ref,
                y_ref, tx_ref, tb_ref, tc_ref, st_ref,
                xbuf, bbuf, cbuf, state, *, L, valid):
    ci = pl.program_id(2)

    @pl.when(ci == 0)
    def _():
        xbuf[0:SSM_HIST, :] = hx_ref[0]
        bbuf[0:SSM_HIST, :] = hb_ref[0]
        cbuf[0:SSM_HIST, :] = hc_ref[0]
        state[...] = st0_ref[0].T

    def short_conv(buf, new_ref, w_ref, bias_ref, tail_ref):
        buf[SSM_HIST:SSM_HIST + L, :] = new_ref[...].astype(F32)
        off = SSM_HIST - (SSM_CONV_K - 1)
        acc = buf[off:off + L, :] * w_ref[0, 0:1, :]
        for k in range(1, SSM_CONV_K):
            acc = acc + buf[off + k:off + k + L, :] * w_ref[0, k:k + 1, :]
        tail = buf[valid:valid + SSM_HIST, :]
        tail_ref[0] = tail
        buf[0:SSM_HIST, :] = tail
        return _silu(acc + bias_ref[0])

    x = short_conv(xbuf, x_ref, cwx_ref, cbx_ref, tx_ref)
    bm = short_conv(bbuf, b_ref, cwb_ref, cbb_ref, tb_ref)
    cm = short_conv(cbuf, c_ref, cwc_ref, cbc_ref, tc_ref)

    row = lax.broadcasted_iota(jnp.int32, (L, L), 0)
    col = lax.broadcasted_iota(jnp.int32, (L, L), 1)
    causal = col <= row
    dt = _softplus(dt_ref[...] + dtb_ref[0])
    if valid < L:
        dt = jnp.where(lax.broadcasted_iota(jnp.int32, dt.shape, 0) < valid, dt, 0.0)
    da = dt * (-jnp.exp(alog_ref[0]))
    acs = jnp.dot(causal.astype(F32), da, precision=lax.Precision.HIGHEST, preferred_element_type=F32)
    acs_last = acs[L - 1:L, :]
    eacs = jnp.exp(acs)
    dte = jnp.exp(acs_last - acs) * dt
    acs_t = acs.T
    dt_t = dt.T

    head_of_lane = lax.broadcasted_iota(jnp.int32, (LANE, SSM_GROUP_W), 1) // SSM_HEAD_DIM
    expand = (head_of_lane == lax.broadcasted_iota(jnp.int32, (LANE, SSM_GROUP_W), 0))
    eacs_x = _dot(eacs.astype(BF16), expand.astype(BF16))
    dte_x = _dot(dte.astype(BF16), expand.astype(BF16))
    cdec_x = jnp.dot(jnp.broadcast_to(jnp.exp(acs_last), (8, LANE)), expand.astype(F32),
                     precision=lax.Precision.HIGHEST, preferred_element_type=F32)[0:1, :]

    x_b = x.astype(BF16)
    bm_b = bm.astype(BF16)
    cm_b = cm.astype(BF16)
    cb = _dot_nt(cm_b, bm_b)
    st = state[...]
    y_off = _dot(cm_b, st.astype(BF16)) * eacs_x

    low_half = lax.broadcasted_iota(jnp.int32, (L, LANE), 1) < SSM_HEAD_DIM
    pairs = []
    for pr in range(SSM_GROUP_HEADS // 2):
        xp = x_b[:, pr * LANE:(pr + 1) * LANE]
        ys = []
        for e in (2 * pr, 2 * pr + 1):
            seg = jnp.where(causal, jnp.exp(acs[:, e:e + 1] - acs_t[e:e + 1, :]), 0.0)
            mm = (cb * seg * dt_t[e:e + 1, :]).astype(BF16)
            ys.append(_dot(mm, xp))
        pairs.append(jnp.where(low_half, ys[0], ys[1]))
    y = jnp.concatenate(pairs, axis=1) + y_off + dskip_ref[0] * x

    state[...] = st * cdec_x + _dot(bm.T.astype(BF16), (x * dte_x).astype(BF16))

    yz = y * _silu(z_ref[...].astype(F32))
    yn = (yz * lax.rsqrt(jnp.mean(yz * yz, axis=-1, keepdims=True) + EPS)) * gn_ref[0]
    y_ref[...] = yn.astype(y_ref.dtype)
    st_ref[0] = state[...].T


def ssd_mixer(proj, dtp, hist_x, hist_b, hist_c, st0, prm, *, batch, nchunks, L, valid, name):
    gw, ns = SSM_GROUP_W, D_STATE
    zb = 0
    xb = (SSM_GROUPS * gw) // gw
    bb = (2 * SSM_GROUPS * gw) // ns
    cb = bb + SSM_GROUPS
    rows = batch * nchunks * L
    rmap = lambda off: (lambda b, g, c: (b * nchunks + c, off + g))
    hmap = lambda b, g, c: (b, 0, g)
    pmap = lambda b, g, c: (g, 0, 0)
    in_specs = [
        pl.BlockSpec((L, gw), rmap(zb)), pl.BlockSpec((L, gw), rmap(xb)),
        pl.BlockSpec((L, ns), rmap(bb)), pl.BlockSpec((L, ns), rmap(cb)),
        pl.BlockSpec((L, LANE), rmap(0)),
        pl.BlockSpec((1, SSM_HIST, gw), hmap), pl.BlockSpec((1, SSM_HIST, ns), hmap),
        pl.BlockSpec((1, SSM_HIST, ns), hmap),
        pl.BlockSpec((1, gw, ns), lambda b, g, c: (b, g, 0)),
        pl.BlockSpec((1, SSM_CONV_K, gw), pmap), pl.BlockSpec((1, SSM_CONV_K, ns), pmap),
        pl.BlockSpec((1, SSM_CONV_K, ns), pmap),
        pl.BlockSpec((1, 1, gw), pmap), pl.BlockSpec((1, 1, ns), pmap), pl.BlockSpec((1, 1, ns), pmap),
        pl.BlockSpec((1, 1, LANE), pmap), pl.BlockSpec((1, 1, LANE), pmap),
        pl.BlockSpec((1, 1, gw), pmap), pl.BlockSpec((1, 1, gw), pmap),
    ]
    out_specs = [
        pl.BlockSpec((L, gw), rmap(0)),
        pl.BlockSpec((1, SSM_HIST, gw), hmap), pl.BlockSpec((1, SSM_HIST, ns), hmap),
        pl.BlockSpec((1, SSM_HIST, ns), hmap),
        pl.BlockSpec((1, gw, ns), lambda b, g, c: (b, g, 0)),
    ]
    out_shape = [
        jax.ShapeDtypeStruct((rows, SSM_GROUPS * gw), BF16),
        jax.ShapeDtypeStruct((batch, SSM_HIST, SSM_GROUPS * gw), F32),
        jax.ShapeDtypeStruct((batch, SSM_HIST, SSM_GROUPS * ns), F32),
        jax.ShapeDtypeStruct((batch, SSM_HIST, SSM_GROUPS * ns), F32),
        jax.ShapeDtypeStruct((batch, SSM_GROUPS * gw, ns), F32),
    ]
    return pl.pallas_call(
        functools.partial(_ssd_kernel, L=L, valid=valid),
        grid=(batch, SSM_GROUPS, nchunks),
        in_specs=in_specs, out_specs=out_specs, out_shape=out_shape,
        scratch_shapes=[pltpu.VMEM((SSM_HIST + L, gw), F32), pltpu.VMEM((SSM_HIST + L, ns), F32),
                        pltpu.VMEM((SSM_HIST + L, ns), F32), pltpu.VMEM((ns, gw), F32)],
        compiler_params=_cparams("parallel", "parallel", "arbitrary"),
        name=name,
    )(proj, proj, proj, proj, dtp, hist_x, hist_b, hist_c, st0,
      prm["cwx"], prm["cwb"], prm["cwc"], prm["cbx"], prm["cbb"], prm["cbc"],
      prm["dtb"], prm["alog"], prm["dskip"], prm["gn"])


def _rope_tables(pos):
    half = ROPE // 2
    freqs = ROPE_THETA ** (-jnp.arange(half, dtype=F32) / half)
    ang = pos.astype(F32)[:, None] * freqs[None, :]
    zeros = jnp.zeros((pos.shape[0], LANE - ROPE), F32)
    cos, sin = jnp.cos(ang), jnp.sin(ang)
    return jnp.concatenate([cos, cos, zeros], axis=1), jnp.concatenate([sin, sin, zeros], axis=1)


def _rot_cols(w):
    half = ROPE // 2
    return jnp.concatenate([-w[..., half:], w[..., :half]], axis=-1)


def _mla_weights(w_in, w_qb, w_kvb, w_out):
    d = w_in.shape[0]
    qa, ckv = w_in[:, :LORA], w_in[:, LORA:2 * LORA]
    kpe, gate = w_in[:, 2 * LORA:2 * LORA + ROPE], w_in[:, 2 * LORA + ROPE:]
    z = jnp.zeros((d, LANE - ROPE), F32)
    w0 = jnp.concatenate([gate, qa, ckv, kpe, z, _rot_cols(kpe), z], axis=1).astype(BF16)
    hw = HEADS * NOPE
    r = w_qb[:, :, NOPE:]
    zr = jnp.zeros((LORA, HEADS, LANE - ROPE), F32)
    return dict(
        w0=w0,
        wqn=w_qb[:, :, :NOPE].reshape(LORA, hw).astype(BF16),
        wqr=jnp.concatenate([r, zr], axis=-1).reshape(LORA, hw).astype(BF16),
        wqr2=jnp.concatenate([_rot_cols(r), zr], axis=-1).reshape(LORA, hw).astype(BF16),
        wk=w_kvb[:, :, :NOPE].reshape(LORA, hw).astype(BF16),
        wv=w_kvb[:, :, NOPE:].reshape(LORA, hw).astype(BF16),
        wo=w_out.astype(BF16),
    )


def _mla_layer(xp, xs, norm_g, q_norm, kv_norm, wts, ckv_cache, kpe_cache, tabs_p, tabs_s,
               *, batch, seq, fin, tag):
    mid = functools.partial(mla_mid, q_norm=q_norm, kv_norm=kv_norm, wqn=wts["wqn"], wqr=wts["wqr"],
                            wqr2=wts["wqr2"], wk=wts["wk"], wv=wts["wv"])
    proj_p = norm_matmul(xp, norm_g, wts["w0"], tm=512, tn=MLA_PROJ_W, out_dtype=BF16, name=tag + "_inproj_p")
    q, ckv_p, kpe_p, k, v = mid(proj_p, *tabs_p, tm=256, with_kv=True, name=tag + "_mid_p")
    o = flash_attention(q, k, v, batch=batch, seq=seq, tile=512, name=tag + "_flash")
    xp = out_proj(o, proj_p, wts["wo"], xp, fin, tm=256, name=tag + "_outproj_p")

    ms = xs.shape[0]
    proj_s = norm_matmul(xs, norm_g, wts["w0"], tm=ms, tn=MLA_PROJ_W, out_dtype=BF16, name=tag + "_inproj_s")
    qs, ckv_s, kpe_s = mid(proj_s, *tabs_s, tm=ms, with_kv=False, name=tag + "_mid_s")
    os_ = sample_attention(qs, ckv_s, kpe_s, ckv_cache, kpe_cache, wts["wk"], wts["wv"], name=tag + "_attn_s")
    xs = out_proj(os_, proj_s, wts["wo"], xs, fin, tm=ms, name=tag + "_outproj_s")
    return xp, xs, ckv_p, kpe_p, ckv_s, kpe_s


def _ssd_layer(xp, xs, l2_norm, l2_w_in, l2_conv_w, l2_conv_b, l2_dt_bias, l2_a_log, l2_d_skip, l2_gnorm, l2_w_out,
               state_l2_conv, state_l2_ssm, *, batch, seq, nb, new, L=128):
    d = xp.shape[1]
    ms = nb * new
    inner = SSM_GROUPS * SSM_GROUP_W
    bc = SSM_GROUPS * D_STATE
    main_w = 2 * inner + 2 * bc
    w2 = l2_w_in[:, :main_w].astype(BF16)
    w2dt = jnp.pad(l2_w_in[:, main_w:].reshape(d, SSM_GROUPS, SSM_GROUP_HEADS),
                   ((0, 0), (0, 0), (0, LANE - SSM_GROUP_HEADS))).reshape(d, SSM_GROUPS * LANE).astype(BF16)
    w2o = l2_w_out.astype(BF16)
    per_head = lambda v: jnp.pad(v.reshape(SSM_GROUPS, 1, SSM_GROUP_HEADS),
                                 ((0, 0), (0, 0), (0, LANE - SSM_GROUP_HEADS)))
    cw = l2_conv_w
    grp = lambda w, width: w.reshape(w.shape[0], SSM_GROUPS, width).transpose(1, 0, 2)
    prm = dict(
        cwx=grp(cw[:, :inner], SSM_GROUP_W), cwb=grp(cw[:, inner:inner + bc], D_STATE),
        cwc=grp(cw[:, inner + bc:], D_STATE),
        cbx=grp(l2_conv_b[None, :inner], SSM_GROUP_W), cbb=grp(l2_conv_b[None, inner:inner + bc], D_STATE),
        cbc=grp(l2_conv_b[None, inner + bc:], D_STATE),
        dtb=per_head(l2_dt_bias), alog=per_head(l2_a_log),
        dskip=jnp.repeat(l2_d_skip, SSM_HEAD_DIM).reshape(SSM_GROUPS, 1, SSM_GROUP_W),
        gn=l2_gnorm.reshape(SSM_GROUPS, 1, SSM_GROUP_W),
    )
    hist_rows = SSM_CONV_K - 1

    def run_ssd(proj, dtp, hist, st0, nbatch, nchunks, valid, name):
        hist = jnp.pad(hist, ((0, 0), (SSM_HIST - hist_rows, 0), (0, 0)))
        y, tx, tb, tc, st = ssd_mixer(
            proj, dtp, hist[:, :, :inner], hist[:, :, inner:inner + bc], hist[:, :, inner + bc:],
            st0.reshape(nbatch, inner, D_STATE), prm, batch=nbatch, nchunks=nchunks, L=L, valid=valid, name=name)
        tail = jnp.concatenate([tx, tb, tc], axis=-1)[:, SSM_HIST - hist_rows:]
        return y, tail, st.reshape(nbatch, SSM_GROUPS * SSM_GROUP_HEADS, SSM_HEAD_DIM, D_STATE)

    proj_p = norm_matmul(xp, l2_norm, w2, tm=256, tn=main_w // 2, out_dtype=BF16, name="l2_inproj_p")
    dt_p = norm_matmul(xp, l2_norm, w2dt, tm=512, tn=SSM_GROUPS * LANE, out_dtype=F32, name="l2_dtproj_p")
    y_p, l2_conv_p, l2_ssm_p = run_ssd(
        proj_p, dt_p, jnp.zeros((batch, hist_rows, inner + 2 * bc), F32),
        jnp.zeros((batch, inner, D_STATE), F32), batch, seq // L, L, "l2_ssd_p")
    xp = out_proj(y_p, None, w2o, xp, None, tm=256, name="l2_outproj_p")

    proj_s = norm_matmul(xs, l2_norm, w2, tm=ms, tn=main_w // 2, out_dtype=BF16, name="l2_inproj_s")
    dt_s = norm_matmul(xs, l2_norm, w2dt, tm=ms, tn=SSM_GROUPS * LANE, out_dtype=F32, name="l2_dtproj_s")
    pad_rows = lambda a: jnp.pad(a.reshape(nb, new, a.shape[1]), ((0, 0), (0, L - new), (0, 0))).reshape(nb * L, a.shape[1])
    y_s, l2_conv_s, l2_ssm_s = run_ssd(pad_rows(proj_s), pad_rows(dt_s), state_l2_conv, state_l2_ssm,
                                       nb, 1, new, "l2_ssd_s")
    y_s = y_s.reshape(nb, L, inner)[:, :new].reshape(ms, inner)
    xs = out_proj(y_s, None, w2o, xs, None, tm=ms, name="l2_outproj_s")

    return xp, xs, l2_conv_p, l2_ssm_p, l2_conv_s, l2_ssm_s


def kernel(x_prompt, x_sample, cache_l0_ckv, cache_l0_kpe, state_l1_conv, state_l2_conv, state_l2_ssm, cache_l3_ckv, cache_l3_kpe, l0_norm, l0_w_in, l0_q_norm, l0_w_qb, l0_kv_norm, l0_w_kvb, l0_w_out, l1_norm, l1_w_in, l1_dw_w, l1_dw_b, l1_ln_g, l1_ln_b, l1_w_out, l2_norm, l2_w_in, l2_conv_w, l2_conv_b, l2_dt_bias, l2_a_log, l2_d_skip, l2_gnorm, l2_w_out, l3_norm, l3_w_in, l3_q_norm, l3_w_qb, l3_kv_norm, l3_w_kvb, l3_w_out, final_norm):
    batch, seq, d = x_prompt.shape
    nb, new, _ = x_sample.shape
    past = cache_l0_ckv.shape[1]
    xp = x_prompt.reshape(batch * seq, d)
    xs = x_sample.reshape(nb * new, d)
    ms = nb * new

    tabs_p = _rope_tables(jnp.arange(seq))
    cs, sn = _rope_tables(past + jnp.arange(new))
    tabs_s = (jnp.tile(cs, (nb, 1)), jnp.tile(sn, (nb, 1)))

    w0 = _mla_weights(l0_w_in, l0_w_qb, l0_w_kvb, l0_w_out)
    xp, xs, l0_ckv_p, l0_kpe_p, l0_ckv_s, l0_kpe_s = _mla_layer(
        xp, xs, l0_norm, l0_q_norm, l0_kv_norm, w0, cache_l0_ckv, cache_l0_kpe, tabs_p, tabs_s,
        batch=batch, seq=seq, fin=None, tag="l0")

    w1 = l1_w_in.astype(BF16)
    w1o = l1_w_out.astype(BF16)
    pad_hist = lambda h: jnp.pad(h, ((0, 0), (CONV_HIST - (CONV_K - 1), 0), (0, 0)))
    proj_p = norm_matmul(xp, l1_norm, w1, tm=512, tn=w1.shape[1] // 2, out_dtype=BF16, name="l1_inproj_p")
    a_p, tail_p = conv_mixer(proj_p, jnp.zeros((batch, CONV_HIST, d), F32), l1_dw_w, l1_dw_b, l1_ln_g, l1_ln_b,
                             batch=batch, seq=seq, tt=256, name="l1_conv_p")
    xp = out_proj(a_p, None, w1o, xp, None, tm=256, name="l1_outproj_p")
    proj_s = norm_matmul(xs, l1_norm, w1, tm=ms, tn=w1.shape[1] // 2, out_dtype=BF16, name="l1_inproj_s")
    a_s, tail_s = conv_mixer(proj_s, pad_hist(state_l1_conv), l1_dw_w, l1_dw_b, l1_ln_g, l1_ln_b,
                             batch=nb, seq=new, tt=new, name="l1_conv_s")
    xs = out_proj(a_s, None, w1o, xs, None, tm=ms, name="l1_outproj_s")
    l1_conv_p = tail_p[:, CONV_HIST - (CONV_K - 1):]
    l1_conv_s = tail_s[:, CONV_HIST - (CONV_K - 1):]

    xp, xs, l2_conv_p, l2_ssm_p, l2_conv_s, l2_ssm_s = _ssd_layer(
        xp, xs, l2_norm, l2_w_in, l2_conv_w, l2_conv_b, l2_dt_bias, l2_a_log, l2_d_skip, l2_gnorm, l2_w_out,
        state_l2_conv, state_l2_ssm, batch=batch, seq=seq, nb=nb, new=new)

    w3 = _mla_weights(l3_w_in, l3_w_qb, l3_w_kvb, l3_w_out)
    yp, ys, l3_ckv_p, l3_kpe_p, l3_ckv_s, l3_kpe_s = _mla_layer(
        xp, xs, l3_norm, l3_q_norm, l3_kv_norm, w3, cache_l3_ckv, cache_l3_kpe, tabs_p, tabs_s,
        batch=batch, seq=seq, fin=final_norm, tag="l3")

    r3 = lambda a, n: a.reshape(n, -1, a.shape[-1])
    return (yp.reshape(batch, seq, d), ys.reshape(nb, new, d),
            r3(l0_ckv_p, batch), r3(l0_kpe_p, batch), r3(l0_ckv_s, nb), r3(l0_kpe_s, nb),
            l1_conv_p, l1_conv_s,
            l2_conv_p, l2_ssm_p, l2_conv_s, l2_ssm_s,
            r3(l3_ckv_p, batch), r3(l3_kpe_p, batch), r3(l3_ckv_s, nb), r3(l3_kpe_s, nb))
```

```python
import functools

import jax
import jax.numpy as jnp
from jax import lax
from jax.experimental import pallas as pl
from jax.experimental.pallas import tpu as pltpu

F32 = jnp.float32
BF16 = jnp.bfloat16

EPS = 1e-6
ROPE_THETA = 10000.0
CHUNK = 64
HEADS = 16
NOPE = 128
ROPE = 64
HEAD_PAD = 256
MLA_SCALE = (NOPE + ROPE) ** -0.5
LOG2E = 1.4426950408889634
LORA = 512
CONV_K = 31
CONV_HIST = 32
SSM_GROUPS = 8
SSM_GROUP_HEADS = 8
SSM_HEAD_DIM = 64
SSM_GROUP_W = SSM_GROUP_HEADS * SSM_HEAD_DIM
D_STATE = 128
SSM_CONV_K = 4
SSM_HIST = 8
LANE = 128

VMEM_LIMIT_BYTES = 56 * 1024 * 1024


def _cparams(*sem):
    return pltpu.CompilerParams(dimension_semantics=sem, vmem_limit_bytes=VMEM_LIMIT_BYTES)


def _resident(shape, index_map):
    return pl.BlockSpec(shape, index_map, pipeline_mode=pl.Buffered(1))


def _sigmoid(x):
    return 0.5 * jnp.tanh(0.5 * x) + 0.5


def _silu(x):
    h = 0.5 * x
    return h + h * jnp.tanh(h)


def _split3(x):
    hi = x.astype(BF16)
    r = x - hi.astype(F32)
    mid = r.astype(BF16)
    return hi, mid, (r - mid.astype(F32)).astype(BF16)


def _dot_onehot_lhs(sel, x):
    hi, mid, lo = _split3(x)
    return _dot(sel, hi) + (_dot(sel, mid) + _dot(sel, lo))


def _dot_onehot_rhs(x, sel):
    hi, mid, lo = _split3(x)
    return _dot(hi, sel) + (_dot(mid, sel) + _dot(lo, sel))


def _rms(x, g):
    return (x * lax.rsqrt(jnp.mean(x * x, axis=-1, keepdims=True) + EPS)) * g


def _dot(a, b):
    return jnp.dot(a, b, preferred_element_type=F32)


def _dot_nt(a, b):
    return lax.dot_general(a, b, (((1,), (1,)), ((), ())), preferred_element_type=F32)


def _norm_matmul_kernel(x_ref, g_ref, w_ref, o_ref, *, col_chunk):
    xn = _rms(x_ref[...], g_ref[...]).astype(BF16)
    for c in range(0, o_ref.shape[1], col_chunk):
        o_ref[:, c:c + col_chunk] = _dot(xn, w_ref[:, c:c + col_chunk]).astype(o_ref.dtype)


def norm_matmul(x, g, w, *, tm, tn, out_dtype, name):
    m, k = x.shape
    n = w.shape[1]
    col_chunk = 256 if tn % 256 == 0 else LANE
    return pl.pallas_call(
        functools.partial(_norm_matmul_kernel, col_chunk=col_chunk),
        grid=(n // tn, m // tm),
        in_specs=[pl.BlockSpec((tm, k), lambda j, i: (i, 0)),
                  _resident((1, k), lambda j, i: (0, 0)),
                  _resident((k, tn), lambda j, i: (0, j))],
        out_specs=pl.BlockSpec((tm, tn), lambda j, i: (i, j)),
        out_shape=jax.ShapeDtypeStruct((m, n), out_dtype),
        compiler_params=_cparams("arbitrary", "arbitrary"),
        name=name,
    )(x, g.reshape(1, k), w)


def _out_proj_kernel(*refs, gated, layer_norm, final):
    refs = list(refs)
    a_ref = refs.pop(0)
    gate_ref = refs.pop(0) if gated else None
    lng_ref, lnb_ref = (refs.pop(0), refs.pop(0)) if layer_norm else (None, None)
    w_ref, res_ref = refs.pop(0), refs.pop(0)
    fin_ref = refs.pop(0) if final else None
    o_ref = refs.pop(0)
    a = a_ref[...]
    if layer_norm:
        cen = a - jnp.mean(a, axis=-1, keepdims=True)
        var = jnp.mean(cen * cen, axis=-1, keepdims=True)
        a = _silu((cen * lax.rsqrt(var + EPS)) * lng_ref[...] + lnb_ref[...])
    if gated:
        a = a.astype(F32) * _silu(gate_ref[...].astype(F32))
    y = res_ref[...] + _dot(a.astype(BF16), w_ref[...])
    if final:
        y = _rms(y, fin_ref[...])
    o_ref[...] = y


def out_proj(a, gate_src, w, res, fin, *, tm, name, gate_block=0, ln=None):
    m, k = a.shape
    n = w.shape[1]
    gated, final, layer_norm = gate_src is not None, fin is not None, ln is not None
    in_specs = [pl.BlockSpec((tm, k), lambda i: (i, 0))]
    args = [a]
    if gated:
        in_specs.append(pl.BlockSpec((tm, k), lambda i: (i, gate_block)))
        args.append(gate_src)
    if layer_norm:
        in_specs += [_resident((1, k), lambda i: (0, 0)), _resident((1, k), lambda i: (0, 0))]
        args += [ln[0].reshape(1, k), ln[1].reshape(1, k)]
    in_specs += [_resident((k, n), lambda i: (0, 0)), pl.BlockSpec((tm, n), lambda i: (i, 0))]
    args += [w, res]
    if final:
        in_specs.append(_resident((1, n), lambda i: (0, 0)))
        args.append(fin.reshape(1, n))
    return pl.pallas_call(
        functools.partial(_out_proj_kernel, gated=gated, layer_norm=layer_norm, final=final),
        grid=(m // tm,),
        in_specs=in_specs,
        out_specs=pl.BlockSpec((tm, n), lambda i: (i, 0)),
        out_shape=jax.ShapeDtypeStruct((m, n), F32),
        compiler_params=_cparams("parallel"),
        name=name,
    )(*args)


MLA_GATE_W = HEADS * NOPE
MLA_PROJ_W = MLA_GATE_W + 2 * LORA + 2 * LANE


def _mla_mid_kernel(*refs, with_kv):
    (qa_ref, ckv_ref, kpe_ref, cos_ref, sin_ref, qn_ref, kvn_ref,
     wqn_ref, wqr_ref, wqr2_ref) = refs[:10]
    refs = refs[10:]
    if with_kv:
        wk_ref, wv_ref, q_out, ckv_out, kpe_out, k_out, v_out = refs
    else:
        q_out, ckv_out, kpe_out = refs
    cosp, sinp = cos_ref[...], sin_ref[...]
    q_scale = MLA_SCALE * LOG2E if with_kv else MLA_SCALE

    qa = _rms(qa_ref[...].astype(F32), qn_ref[...]).astype(BF16)
    qn = _dot(qa, wqn_ref[...]) * q_scale
    qr = _dot(qa, wqr_ref[...])
    qr2 = _dot(qa, wqr2_ref[...])
    for h in range(HEADS):
        lo = h * LANE
        q_out[:, 2 * lo:2 * lo + LANE] = qn[:, lo:lo + LANE].astype(BF16)
        rot = (qr[:, lo:lo + LANE] * cosp + qr2[:, lo:lo + LANE] * sinp) * q_scale
        q_out[:, 2 * lo + LANE:2 * lo + 2 * LANE] = rot.astype(BF16)

    ckv = _rms(ckv_ref[...].astype(F32), kvn_ref[...])
    ckv_out[...] = ckv
    kp = kpe_ref[...].astype(F32)
    kpe = kp[:, :LANE] * cosp + kp[:, LANE:] * sinp
    kpe_out[...] = kpe[:, :ROPE]
    if with_kv:
        ckv_b = ckv.astype(BF16)
        kpe_b = kpe.astype(BF16)
        kn = _dot(ckv_b, wk_ref[...])
        for h in range(HEADS):
            lo = h * LANE
            k_out[:, 2 * lo:2 * lo + LANE] = kn[:, lo:lo + LANE].astype(BF16)
            k_out[:, 2 * lo + LANE:2 * lo + 2 * LANE] = kpe_b
        v_out[...] = _dot(ckv_b, wv_ref[...]).astype(BF16)


def mla_mid(proj, cos_t, sin_t, q_norm, kv_norm, wqn, wqr, wqr2, wk, wv, *, tm, with_kv, name):
    m = proj.shape[0]
    nt = cos_t.shape[0] // tm
    hw = HEADS * LANE
    in_specs = [
        pl.BlockSpec((tm, LORA), lambda i: (i, MLA_GATE_W // LORA)),
        pl.BlockSpec((tm, LORA), lambda i: (i, MLA_GATE_W // LORA + 1)),
        pl.BlockSpec((tm, 2 * LANE), lambda i: (i, (MLA_GATE_W + 2 * LORA) // (2 * LANE))),
        pl.BlockSpec((tm, LANE), lambda i: (i % nt, 0)),
        pl.BlockSpec((tm, LANE), lambda i: (i % nt, 0)),
        _resident((1, LORA), lambda i: (0, 0)),
        _resident((1, LORA), lambda i: (0, 0)),
        _resident((LORA, hw), lambda i: (0, 0)),
        _resident((LORA, hw), lambda i: (0, 0)),
        _resident((LORA, hw), lambda i: (0, 0)),
    ]
    args = [proj, proj, proj, cos_t, sin_t, q_norm.reshape(1, LORA), kv_norm.reshape(1, LORA), wqn, wqr, wqr2]
    out_specs = [pl.BlockSpec((tm, HEADS * HEAD_PAD), lambda i: (i, 0)),
                 pl.BlockSpec((tm, LORA), lambda i: (i, 0)),
                 pl.BlockSpec((tm, ROPE), lambda i: (i, 0))]
    out_shape = [jax.ShapeDtypeStruct((m, HEADS * HEAD_PAD), BF16),
                 jax.ShapeDtypeStruct((m, LORA), F32),
                 jax.ShapeDtypeStruct((m, ROPE), F32)]
    if with_kv:
        in_specs += [_resident((LORA, hw), lambda i: (0, 0)), _resident((LORA, hw), lambda i: (0, 0))]
        args += [wk, wv]
        out_specs += [pl.BlockSpec((tm, HEADS * HEAD_PAD), lambda i: (i, 0)),
                      pl.BlockSpec((tm, hw), lambda i: (i, 0))]
        out_shape += [jax.ShapeDtypeStruct((m, HEADS * HEAD_PAD), BF16),
                      jax.ShapeDtypeStruct((m, hw), BF16)]
    return pl.pallas_call(
        functools.partial(_mla_mid_kernel, with_kv=with_kv),
        grid=(m // tm,),
        in_specs=in_specs, out_specs=out_specs, out_shape=out_shape,
        compiler_params=_cparams("parallel"),
        name=name,
    )(*args)


NEG_BIG = -1e30


FLASH_HEADS = 2


def _flash_kernel(q_ref, k_ref, v_ref, o_ref, *, tile):
    qi = pl.program_id(2)

    def step(ki, carry, masked):
        start = pl.multiple_of(ki * tile, tile)
        out = []
        for h in range(FLASH_HEADS):
            m_prev, l_prev, acc = carry[h]
            q = q_ref[:, h * HEAD_PAD:(h + 1) * HEAD_PAD]
            s = _dot_nt(q, k_ref[pl.ds(start, tile), h * HEAD_PAD:(h + 1) * HEAD_PAD])
            if masked:
                r = lax.broadcasted_iota(jnp.int32, s.shape, 0) // CHUNK
                c = lax.broadcasted_iota(jnp.int32, s.shape, 1) // CHUNK
                s = jnp.where(c <= r, s, NEG_BIG)
            m_new = jnp.maximum(m_prev, jnp.max(s, axis=-1, keepdims=True))
            alpha = jnp.exp2(m_prev - m_new)
            p = jnp.exp2(s - m_new)
            l_new = alpha * l_prev + jnp.sum(p, axis=-1, keepdims=True)
            pv = _dot(p.astype(BF16), v_ref[pl.ds(start, tile), h * NOPE:(h + 1) * NOPE])
            out.append((m_new, l_new, alpha * acc + pv))
        return tuple(out)

    init = tuple((jnp.full((tile, 1), NEG_BIG, F32), jnp.zeros((tile, 1), F32), jnp.zeros((tile, NOPE), F32))
                 for _ in range(FLASH_HEADS))
    carry = lax.fori_loop(0, qi, lambda ki, c: step(ki, c, False), init)
    carry = step(qi, carry, True)
    for h in range(FLASH_HEADS):
        _, l_fin, acc = carry[h]
        o_ref[:, h * NOPE:(h + 1) * NOPE] = (acc / l_fin).astype(o_ref.dtype)


def flash_attention(q, k, v, *, batch, seq, tile, name):
    nq = seq // tile
    hq, hv = FLASH_HEADS * HEAD_PAD, FLASH_HEADS * NOPE
    return pl.pallas_call(
        functools.partial(_flash_kernel, tile=tile),
        grid=(batch, HEADS // FLASH_HEADS, nq),
        in_specs=[pl.BlockSpec((tile, hq), lambda b, h, i: (b * nq + i, h)),
                  pl.BlockSpec((seq, hq), lambda b, h, i: (b, h)),
                  pl.BlockSpec((seq, hv), lambda b, h, i: (b, h))],
        out_specs=pl.BlockSpec((tile, hv), lambda b, h, i: (b * nq + i, h)),
        out_shape=jax.ShapeDtypeStruct((batch * seq, HEADS * NOPE), BF16),
        compiler_params=_cparams("parallel", "parallel", "arbitrary"),
        name=name,
    )(q, k, v)


def _sample_attn_kernel(q_ref, cnew_ref, pnew_ref, ccache_ref, pcache_ref, wk_ref, wv_ref, o_ref,
                        kc_scr, kp_scr, *, past, new, pad):
    total = past + pad
    kc_scr[0:past, :] = ccache_ref[0].astype(BF16)
    kp_scr[0:past, :] = pcache_ref[0].astype(BF16)
    kc_scr[past:total, :] = jnp.zeros((pad, LORA), BF16)
    kp_scr[past:total, :] = jnp.zeros((pad, ROPE), BF16)
    kc_scr[past:past + new, :] = cnew_ref[...].astype(BF16)
    kp_scr[past:past + new, :] = pnew_ref[...].astype(BF16)

    q = q_ref[...]
    qlat, qpe = [], []
    for h in range(HEADS):
        lo = h * HEAD_PAD
        qlat.append(_dot_nt(q[:, lo:lo + NOPE], wk_ref[:, h * NOPE:(h + 1) * NOPE]))
        qpe.append(q[:, lo + NOPE:lo + NOPE + ROPE])
    qlat = jnp.concatenate(qlat, axis=0).astype(BF16)
    qpe = jnp.concatenate(qpe, axis=0)
    kc = kc_scr[...]
    s = _dot_nt(qlat, kc) + _dot_nt(qpe, kp_scr[...])
    col = lax.broadcasted_iota(jnp.int32, s.shape, 1)
    s = jnp.where(col < past + new, s, NEG_BIG)
    m = jnp.max(s, axis=-1, keepdims=True)
    p = jnp.exp(s - m)
    l = jnp.sum(p, axis=-1, keepdims=True)
    olat = (_dot(p.astype(BF16), kc) / l).astype(BF16)
    for h in range(HEADS):
        o_ref[:, h * NOPE:(h + 1) * NOPE] = _dot(
            olat[h * new:(h + 1) * new, :], wv_ref[:, h * NOPE:(h + 1) * NOPE]).astype(o_ref.dtype)


def sample_attention(q, ckv_new, kpe_new, ckv_cache, kpe_cache, wk, wv, *, name):
    nb, past, _ = ckv_cache.shape
    new = q.shape[0] // nb
    pad = LANE
    hw = HEADS * NOPE
    return pl.pallas_call(
        functools.partial(_sample_attn_kernel, past=past, new=new, pad=pad),
        grid=(nb,),
        in_specs=[pl.BlockSpec((new, HEADS * HEAD_PAD), lambda b: (b, 0)),
                  pl.BlockSpec((new, LORA), lambda b: (b, 0)),
                  pl.BlockSpec((new, ROPE), lambda b: (b, 0)),
                  pl.BlockSpec((1, past, LORA), lambda b: (b, 0, 0)),
                  pl.BlockSpec((1, past, ROPE), lambda b: (b, 0, 0)),
                  _resident((LORA, hw), lambda b: (0, 0)),
                  _resident((LORA, hw), lambda b: (0, 0))],
        out_specs=pl.BlockSpec((new, hw), lambda b: (b, 0)),
        out_shape=jax.ShapeDtypeStruct((nb * new, hw), BF16),
        scratch_shapes=[pltpu.VMEM((past + pad, LORA), BF16), pltpu.VMEM((past + pad, ROPE), BF16)],
        compiler_params=_cparams("arbitrary"),
        name=name,
    )(q, ckv_new, kpe_new, ckv_cache, kpe_cache, wk, wv)


SUBLANES = 8
CONV_COLS = 512


def _conv_kernel(val_ref, glu_ref, hist_ref, dw_ref, dwb_ref, y_ref, tail_ref, ubuf, shifted, *, tt):
    i = pl.program_id(2)

    @pl.when(i == 0)
    def _():
        ubuf[0:CONV_HIST, :] = hist_ref[0]

    ubuf[CONV_HIST:CONV_HIST + tt, :] = val_ref[...].astype(F32) * _sigmoid(glu_ref[...].astype(F32))
    span = tt + CONV_HIST - SUBLANES
    for r in range(1, SUBLANES):
        shifted[r - 1] = ubuf[r:r + span, :]
    first = CONV_HIST - (CONV_K - 1)
    for c in range(0, ubuf.shape[1], LANE):
        acc = None
        for k in range(CONV_K):
            q, r = divmod(first + k, SUBLANES)
            rows = pl.ds(q * SUBLANES, tt)
            src = ubuf[rows, c:c + LANE] if r == 0 else shifted[r - 1, rows, c:c + LANE]
            term = src * dw_ref[k:k + 1, c:c + LANE]
            acc = term if acc is None else acc + term
        y_ref[:, c:c + LANE] = acc + dwb_ref[:, c:c + LANE]
    tail = ubuf[tt:tt + CONV_HIST, :]
    tail_ref[0] = tail
    ubuf[0:CONV_HIST, :] = tail


def conv_mixer(proj, hist, dw_w, dw_b, *, batch, seq, tt, name):
    d = dw_w.shape[1]
    nt, cw = seq // tt, CONV_COLS
    nc = d // cw
    return pl.pallas_call(
        functools.partial(_conv_kernel, tt=tt),
        grid=(batch, nc, nt),
        in_specs=[pl.BlockSpec((tt, cw), lambda b, j, i: (b * nt + i, j)),
                  pl.BlockSpec((tt, cw), lambda b, j, i: (b * nt + i, nc + j)),
                  pl.BlockSpec((1, CONV_HIST, cw), lambda b, j, i: (b, 0, j)),
                  pl.BlockSpec((CONV_K, cw), lambda b, j, i: (0, j)),
                  pl.BlockSpec((1, cw), lambda b, j, i: (0, j))],
        out_specs=[pl.BlockSpec((tt, cw), lambda b, j, i: (b * nt + i, j)),
                   pl.BlockSpec((1, CONV_HIST, cw), lambda b, j, i: (b, 0, j))],
        out_shape=[jax.ShapeDtypeStruct((batch * seq, d), F32),
                   jax.ShapeDtypeStruct((batch, CONV_HIST, d), F32)],
        scratch_shapes=[pltpu.VMEM((CONV_HIST + tt, cw), F32),
                        pltpu.VMEM((SUBLANES - 1, tt + CONV_HIST - SUBLANES, cw), F32)],
        compiler_params=_cparams("parallel", "parallel", "arbitrary"),
        name=name,
    )(proj, proj, hist, dw_w, dw_b.reshape(1, d))


def _softplus(x):
    return jnp.maximum(x, 0.0) + jnp.log1p(jnp.exp(-jnp.abs(x)))


SSD_GROUPS_PER_STEP = 1


def _ssd_kernel(z_ref, x_ref, b_ref, c_ref, dt_ref, hx_ref, hb_ref, hc_ref, st0_ref,
                cwx_ref, cwb_ref, cwc_ref, cbx_ref, cbb_ref, cbc_ref,
                dtb_ref, alog_ref, dskip_ref, gn_ref, expb_ref,
                y_ref, tx_ref, tb_ref, tc_ref, st_ref,
                xbuf, bbuf, cbuf, state, *, L, valid):
    gw, ns = SSM_GROUP_W, D_STATE
    for gi in range(SSD_GROUPS_PER_STEP):
        ch, sn, hd, one = pl.ds(gi * gw, gw), pl.ds(gi * ns, ns), pl.ds(gi * LANE, LANE), pl.ds(gi, 1)
        _ssd_group(
            z_ref.at[:, ch], x_ref.at[:, ch], b_ref.at[:, sn], c_ref.at[:, sn], dt_ref.at[:, hd],
            hx_ref.at[:, :, ch], hb_ref.at[:, :, sn], hc_ref.at[:, :, sn], st0_ref.at[:, ch, :],
            cwx_ref.at[one], cwb_ref.at[one], cwc_ref.at[one], cbx_ref.at[one], cbb_ref.at[one], cbc_ref.at[one],
            dtb_ref.at[one], alog_ref.at[one], dskip_ref.at[one], gn_ref.at[one], expb_ref,
            y_ref.at[:, ch], tx_ref.at[:, :, ch], tb_ref.at[:, :, sn], tc_ref.at[:, :, sn], st_ref.at[:, ch, :],
            xbuf.at[:, ch], bbuf.at[:, sn], cbuf.at[:, sn], state.at[gi], L=L, valid=valid)


def _ssd_group(z_ref, x_ref, b_ref, c_ref, dt_ref, hx_ref, hb_ref, hc_ref, st0_ref,
               cwx_ref, cwb_ref, cwc_ref, cbx_ref, cbb_ref, cbc_ref,
               dtb_ref, alog_ref, dskip_ref, gn_ref, expb_ref,
               y_ref, tx_ref, tb_ref, tc_ref, st_ref,
               xbuf, bbuf, cbuf, state, *, L, valid):
    ci = pl.program_id(2)

    @pl.when(ci == 0)
    def _():
        xbuf[0:SSM_HIST, :] = hx_ref[0]
        bbuf[0:SSM_HIST, :] = hb_ref[0]
        cbuf[0:SSM_HIST, :] = hc_ref[0]
        state[...] = st0_ref[0].T

    def short_conv(buf, new_ref, w_ref, bias_ref, tail_ref):
        buf[SSM_HIST:SSM_HIST + L, :] = new_ref[...].astype(F32)
        off = SSM_HIST - (SSM_CONV_K - 1)
        acc = buf[off:off + L, :] * w_ref[0, 0:1, :]
        for k in range(1, SSM_CONV_K):
            acc = acc + buf[off + k:off + k + L, :] * w_ref[0, k:k + 1, :]
        tail = buf[valid:valid + SSM_HIST, :]
        tail_ref[0] = tail
        buf[0:SSM_HIST, :] = tail
        return _silu(acc + bias_ref[0])

    x = short_conv(xbuf, x_ref, cwx_ref, cbx_ref, tx_ref)
    bm = short_conv(bbuf, b_ref, cwb_ref, cbb_ref, tb_ref)
    cm = short_conv(cbuf, c_ref, cwc_ref, cbc_ref, tc_ref)

    row = lax.broadcasted_iota(jnp.int32, (L, L), 0)
    col = lax.broadcasted_iota(jnp.int32, (L, L), 1)
    causal = col <= row
    dt = _softplus(dt_ref[...] + dtb_ref[0])
    if valid < L:
        dt = jnp.where(lax.broadcasted_iota(jnp.int32, dt.shape, 0) < valid, dt, 0.0)
    da = dt * (-jnp.exp(alog_ref[0]))
    acs = _dot_onehot_lhs(causal.astype(BF16), da)
    acs_last = acs[L - 1:L, :]
    eacs = jnp.exp(acs)
    dte = jnp.exp(acs_last - acs) * dt
    acs_t = acs.T
    dt_t = dt.T

    eacs_x = _dot(eacs.astype(BF16), expb_ref[...])
    dte_x = _dot(dte.astype(BF16), expb_ref[...])
    cdec_x = _dot_onehot_rhs(jnp.broadcast_to(jnp.exp(acs_last), (SUBLANES, LANE)), expb_ref[...])[0:1, :]

    x_b = x.astype(BF16)
    bm_b = bm.astype(BF16)
    cm_b = cm.astype(BF16)
    cb = _dot_nt(cm_b, bm_b)
    st = state[...]
    y_off = _dot(cm_b, st.astype(BF16)) * eacs_x

    low_half = lax.broadcasted_iota(jnp.int32, (L, LANE), 1) < SSM_HEAD_DIM
    pairs = []
    for pr in range(SSM_GROUP_HEADS // 2):
        xp = x_b[:, pr * LANE:(pr + 1) * LANE]
        ys = []
        for e in (2 * pr, 2 * pr + 1):
            seg = jnp.where(causal, jnp.exp(acs[:, e:e + 1] - acs_t[e:e + 1, :]), 0.0)
            mm = (cb * seg * dt_t[e:e + 1, :]).astype(BF16)
            ys.append(_dot(mm, xp))
        pairs.append(jnp.where(low_half, ys[0], ys[1]))
    y = jnp.concatenate(pairs, axis=1) + y_off + dskip_ref[0] * x

    state[...] = st * cdec_x + _dot(bm.T.astype(BF16), (x * dte_x).astype(BF16))

    yz = y * _silu(z_ref[...].astype(F32))
    yn = (yz * lax.rsqrt(jnp.mean(yz * yz, axis=-1, keepdims=True) + EPS)) * gn_ref[0]
    y_ref[...] = yn.astype(y_ref.dtype)
    st_ref[0] = state[...].T


def ssd_mixer(proj, dtp, hist_x, hist_b, hist_c, st0, prm, *, batch, nchunks, L, valid, name):
    gp = SSD_GROUPS_PER_STEP
    gw, ns = SSM_GROUP_W, D_STATE
    sw, sn, sl = gp * gw, gp * ns, gp * LANE
    xb = (SSM_GROUPS * gw) // sw
    bb = (2 * SSM_GROUPS * gw) // sn
    cb = bb + SSM_GROUPS // gp
    rows = batch * nchunks * L
    rmap = lambda off: (lambda b, g, c: (b * nchunks + c, off + g))
    hmap = lambda b, g, c: (b, 0, g)
    pmap = lambda b, g, c: (g, 0, 0)
    in_specs = [
        pl.BlockSpec((L, sw), rmap(0)), pl.BlockSpec((L, sw), rmap(xb)),
        pl.BlockSpec((L, sn), rmap(bb)), pl.BlockSpec((L, sn), rmap(cb)),
        pl.BlockSpec((L, sl), rmap(0)),
        pl.BlockSpec((1, SSM_HIST, sw), hmap), pl.BlockSpec((1, SSM_HIST, sn), hmap),
        pl.BlockSpec((1, SSM_HIST, sn), hmap),
        pl.BlockSpec((1, sw, ns), lambda b, g, c: (b, g, 0)),
        pl.BlockSpec((gp, SSM_CONV_K, gw), pmap), pl.BlockSpec((gp, SSM_CONV_K, ns), pmap),
        pl.BlockSpec((gp, SSM_CONV_K, ns), pmap),
        pl.BlockSpec((gp, 1, gw), pmap), pl.BlockSpec((gp, 1, ns), pmap), pl.BlockSpec((gp, 1, ns), pmap),
        pl.BlockSpec((gp, 1, LANE), pmap), pl.BlockSpec((gp, 1, LANE), pmap),
        pl.BlockSpec((gp, 1, gw), pmap), pl.BlockSpec((gp, 1, gw), pmap),
        _resident((LANE, gw), lambda b, g, c: (0, 0)),
    ]
    expand = (jnp.arange(gw)[None, :] // SSM_HEAD_DIM) == jnp.arange(LANE)[:, None]
    out_specs = [
        pl.BlockSpec((L, sw), rmap(0)),
        pl.BlockSpec((1, SSM_HIST, sw), hmap), pl.BlockSpec((1, SSM_HIST, sn), hmap),
        pl.BlockSpec((1, SSM_HIST, sn), hmap),
        pl.BlockSpec((1, sw, ns), lambda b, g, c: (b, g, 0)),
    ]
    out_shape = [
        jax.ShapeDtypeStruct((rows, SSM_GROUPS * gw), BF16),
        jax.ShapeDtypeStruct((batch, SSM_HIST, SSM_GROUPS * gw), F32),
        jax.ShapeDtypeStruct((batch, SSM_HIST, SSM_GROUPS * ns), F32),
        jax.ShapeDtypeStruct((batch, SSM_HIST, SSM_GROUPS * ns), F32),
        jax.ShapeDtypeStruct((batch, SSM_GROUPS * gw, ns), F32),
    ]
    return pl.pallas_call(
        functools.partial(_ssd_kernel, L=L, valid=valid),
        grid=(batch, SSM_GROUPS // gp, nchunks),
        in_specs=in_specs, out_specs=out_specs, out_shape=out_shape,
        scratch_shapes=[pltpu.VMEM((SSM_HIST + L, sw), F32), pltpu.VMEM((SSM_HIST + L, sn), F32),
                        pltpu.VMEM((SSM_HIST + L, sn), F32), pltpu.VMEM((gp, ns, gw), F32)],
        compiler_params=_cparams("parallel", "parallel", "arbitrary"),
        name=name,
    )(proj, proj, proj, proj, dtp, hist_x, hist_b, hist_c, st0,
      prm["cwx"], prm["cwb"], prm["cwc"], prm["cbx"], prm["cbb"], prm["cbc"],
      prm["dtb"], prm["alog"], prm["dskip"], prm["gn"], expand.astype(BF16))


def _rope_tables(pos):
    half = ROPE // 2
    freqs = ROPE_THETA ** (-jnp.arange(half, dtype=F32) / half)
    ang = pos.astype(F32)[:, None] * freqs[None, :]
    zeros = jnp.zeros((pos.shape[0], LANE - ROPE), F32)
    cos, sin = jnp.cos(ang), jnp.sin(ang)
    return jnp.concatenate([cos, cos, zeros], axis=1), jnp.concatenate([sin, sin, zeros], axis=1)


def _rot_cols(w):
    half = ROPE // 2
    return jnp.concatenate([-w[..., half:], w[..., :half]], axis=-1)


def _mla_weights(w_in, w_qb, w_kvb, w_out):
    d = w_in.shape[0]
    qa, ckv = w_in[:, :LORA], w_in[:, LORA:2 * LORA]
    kpe, gate = w_in[:, 2 * LORA:2 * LORA + ROPE], w_in[:, 2 * LORA + ROPE:]
    z = jnp.zeros((d, LANE - ROPE), F32)
    w0 = jnp.concatenate([gate, qa, ckv, kpe, z, _rot_cols(kpe), z], axis=1).astype(BF16)
    hw = HEADS * NOPE
    r = w_qb[:, :, NOPE:]
    zr = jnp.zeros((LORA, HEADS, LANE - ROPE), F32)
    return dict(
        w0=w0,
        wqn=w_qb[:, :, :NOPE].reshape(LORA, hw).astype(BF16),
        wqr=jnp.concatenate([r, zr], axis=-1).reshape(LORA, hw).astype(BF16),
        wqr2=jnp.concatenate([_rot_cols(r), zr], axis=-1).reshape(LORA, hw).astype(BF16),
        wk=w_kvb[:, :, :NOPE].reshape(LORA, hw).astype(BF16),
        wv=w_kvb[:, :, NOPE:].reshape(LORA, hw).astype(BF16),
        wo=w_out.astype(BF16),
    )


def _mla_layer(xp, xs, norm_g, q_norm, kv_norm, wts, ckv_cache, kpe_cache, tabs_p, tabs_s,
               *, batch, seq, fin, tag):
    mid = functools.partial(mla_mid, q_norm=q_norm, kv_norm=kv_norm, wqn=wts["wqn"], wqr=wts["wqr"],
                            wqr2=wts["wqr2"], wk=wts["wk"], wv=wts["wv"])
    proj_p = norm_matmul(xp, norm_g, wts["w0"], tm=512, tn=MLA_PROJ_W, out_dtype=BF16, name=tag + "_inproj_p")
    q, ckv_p, kpe_p, k, v = mid(proj_p, *tabs_p, tm=256, with_kv=True, name=tag + "_mid_p")
    o = flash_attention(q, k, v, batch=batch, seq=seq, tile=min(seq, 1024), name=tag + "_flash")
    xp = out_proj(o, proj_p, wts["wo"], xp, fin, tm=256, name=tag + "_outproj_p")

    ms = xs.shape[0]
    proj_s = norm_matmul(xs, norm_g, wts["w0"], tm=ms, tn=MLA_PROJ_W, out_dtype=BF16, name=tag + "_inproj_s")
    qs, ckv_s, kpe_s = mid(proj_s, *tabs_s, tm=ms, with_kv=False, name=tag + "_mid_s")
    os_ = sample_attention(qs, ckv_s, kpe_s, ckv_cache, kpe_cache, wts["wk"], wts["wv"], name=tag + "_attn_s")
    xs = out_proj(os_, proj_s, wts["wo"], xs, fin, tm=ms, name=tag + "_outproj_s")
    return xp, xs, ckv_p, kpe_p, ckv_s, kpe_s


def _ssd_layer(xp, xs, l2_norm, l2_w_in, l2_conv_w, l2_conv_b, l2_dt_bias, l2_a_log, l2_d_skip, l2_gnorm, l2_w_out,
               state_l2_conv, state_l2_ssm, *, batch, seq, nb, new, L=128):
    d = xp.shape[1]
    ms = nb * new
    inner = SSM_GROUPS * SSM_GROUP_W
    bc = SSM_GROUPS * D_STATE
    main_w = 2 * inner + 2 * bc
    w2 = l2_w_in[:, :main_w].astype(BF16)
    w2dt = jnp.pad(l2_w_in[:, main_w:].reshape(d, SSM_GROUPS, SSM_GROUP_HEADS),
                   ((0, 0), (0, 0), (0, LANE - SSM_GROUP_HEADS))).reshape(d, SSM_GROUPS * LANE).astype(BF16)
    w2o = l2_w_out.astype(BF16)
    per_head = lambda v: jnp.pad(v.reshape(SSM_GROUPS, 1, SSM_GROUP_HEADS),
                                 ((0, 0), (0, 0), (0, LANE - SSM_GROUP_HEADS)))
    cw = l2_conv_w
    grp = lambda w, width: w.reshape(w.shape[0], SSM_GROUPS, width).transpose(1, 0, 2)
    prm = dict(
        cwx=grp(cw[:, :inner], SSM_GROUP_W), cwb=grp(cw[:, inner:inner + bc], D_STATE),
        cwc=grp(cw[:, inner + bc:], D_STATE),
        cbx=grp(l2_conv_b[None, :inner], SSM_GROUP_W), cbb=grp(l2_conv_b[None, inner:inner + bc], D_STATE),
        cbc=grp(l2_conv_b[None, inner + bc:], D_STATE),
        dtb=per_head(l2_dt_bias), alog=per_head(l2_a_log),
        dskip=jnp.repeat(l2_d_skip, SSM_HEAD_DIM).reshape(SSM_GROUPS, 1, SSM_GROUP_W),
        gn=l2_gnorm.reshape(SSM_GROUPS, 1, SSM_GROUP_W),
    )
    hist_rows = SSM_CONV_K - 1

    def run_ssd(proj, dtp, hist, st0, nbatch, nchunks, valid, name):
        hist = jnp.pad(hist, ((0, 0), (SSM_HIST - hist_rows, 0), (0, 0)))
        y, tx, tb, tc, st = ssd_mixer(
            proj, dtp, hist[:, :, :inner], hist[:, :, inner:inner + bc], hist[:, :, inner + bc:],
            st0.reshape(nbatch, inner, D_STATE), prm, batch=nbatch, nchunks=nchunks, L=L, valid=valid, name=name)
        tail = jnp.concatenate([tx, tb, tc], axis=-1)[:, SSM_HIST - hist_rows:]
        return y, tail, st.reshape(nbatch, SSM_GROUPS * SSM_GROUP_HEADS, SSM_HEAD_DIM, D_STATE)

    proj_p = norm_matmul(xp, l2_norm, w2, tm=256, tn=main_w // 2, out_dtype=BF16, name="l2_inproj_p")
    dt_p = norm_matmul(xp, l2_norm, w2dt, tm=512, tn=SSM_GROUPS * LANE, out_dtype=F32, name="l2_dtproj_p")
    y_p, l2_conv_p, l2_ssm_p = run_ssd(
        proj_p, dt_p, jnp.zeros((batch, hist_rows, inner + 2 * bc), F32),
        jnp.zeros((batch, inner, D_STATE), F32), batch, seq // L, L, "l2_ssd_p")
    xp = out_proj(y_p, None, w2o, xp, None, tm=256, name="l2_outproj_p")

    proj_s = norm_matmul(xs, l2_norm, w2, tm=ms, tn=main_w // 2, out_dtype=BF16, name="l2_inproj_s")
    dt_s = norm_matmul(xs, l2_norm, w2dt, tm=ms, tn=SSM_GROUPS * LANE, out_dtype=F32, name="l2_dtproj_s")
    pad_rows = lambda a: jnp.pad(a.reshape(nb, new, a.shape[1]), ((0, 0), (0, L - new), (0, 0))).reshape(nb * L, a.shape[1])
    y_s, l2_conv_s, l2_ssm_s = run_ssd(pad_rows(proj_s), pad_rows(dt_s), state_l2_conv, state_l2_ssm,
                                       nb, 1, new, "l2_ssd_s")
    y_s = y_s.reshape(nb, L, inner)[:, :new].reshape(ms, inner)
    xs = out_proj(y_s, None, w2o, xs, None, tm=ms, name="l2_outproj_s")

    return xp, xs, l2_conv_p, l2_ssm_p, l2_conv_s, l2_ssm_s


def kernel(x_prompt, x_sample, cache_l0_ckv, cache_l0_kpe, state_l1_conv, state_l2_conv, state_l2_ssm, cache_l3_ckv, cache_l3_kpe, l0_norm, l0_w_in, l0_q_norm, l0_w_qb, l0_kv_norm, l0_w_kvb, l0_w_out, l1_norm, l1_w_in, l1_dw_w, l1_dw_b, l1_ln_g, l1_ln_b, l1_w_out, l2_norm, l2_w_in, l2_conv_w, l2_conv_b, l2_dt_bias, l2_a_log, l2_d_skip, l2_gnorm, l2_w_out, l3_norm, l3_w_in, l3_q_norm, l3_w_qb, l3_kv_norm, l3_w_kvb, l3_w_out, final_norm):
    batch, seq, d = x_prompt.shape
    nb, new, _ = x_sample.shape
    past = cache_l0_ckv.shape[1]
    xp = x_prompt.reshape(batch * seq, d)
    xs = x_sample.reshape(nb * new, d)
    ms = nb * new

    tabs_p = _rope_tables(jnp.arange(seq))
    cs, sn = _rope_tables(past + jnp.arange(new))
    tabs_s = (jnp.tile(cs, (nb, 1)), jnp.tile(sn, (nb, 1)))

    w0 = _mla_weights(l0_w_in, l0_w_qb, l0_w_kvb, l0_w_out)
    xp, xs, l0_ckv_p, l0_kpe_p, l0_ckv_s, l0_kpe_s = _mla_layer(
        xp, xs, l0_norm, l0_q_norm, l0_kv_norm, w0, cache_l0_ckv, cache_l0_kpe, tabs_p, tabs_s,
        batch=batch, seq=seq, fin=None, tag="l0")

    w1 = l1_w_in.astype(BF16)
    w1o = l1_w_out.astype(BF16)
    pad_hist = lambda h: jnp.pad(h, ((0, 0), (CONV_HIST - (CONV_K - 1), 0), (0, 0)))
    proj_p = norm_matmul(xp, l1_norm, w1, tm=512, tn=w1.shape[1] // 2, out_dtype=BF16, name="l1_inproj_p")
    ln1 = (l1_ln_g, l1_ln_b)
    gate_block = 2
    c_p, tail_p = conv_mixer(proj_p, jnp.zeros((batch, CONV_HIST, d), F32), l1_dw_w, l1_dw_b,
                             batch=batch, seq=seq, tt=256, name="l1_conv_p")
    xp = out_proj(c_p, proj_p, w1o, xp, None, tm=256, gate_block=gate_block, ln=ln1, name="l1_outproj_p")
    proj_s = norm_matmul(xs, l1_norm, w1, tm=ms, tn=w1.shape[1] // 2, out_dtype=BF16, name="l1_inproj_s")
    c_s, tail_s = conv_mixer(proj_s, pad_hist(state_l1_conv), l1_dw_w, l1_dw_b,
                             batch=nb, seq=new, tt=new, name="l1_conv_s")
    xs = out_proj(c_s, proj_s, w1o, xs, None, tm=ms, gate_block=gate_block, ln=ln1, name="l1_outproj_s")
    l1_conv_p = tail_p[:, CONV_HIST - (CONV_K - 1):]
    l1_conv_s = tail_s[:, CONV_HIST - (CONV_K - 1):]

    xp, xs, l2_conv_p, l2_ssm_p, l2_conv_s, l2_ssm_s = _ssd_layer(
        xp, xs, l2_norm, l2_w_in, l2_conv_w, l2_conv_b, l2_dt_bias, l2_a_log, l2_d_skip, l2_gnorm, l2_w_out,
        state_l2_conv, state_l2_ssm, batch=batch, seq=seq, nb=nb, new=new)

    w3 = _mla_weights(l3_w_in, l3_w_qb, l3_w_kvb, l3_w_out)
    yp, ys, l3_ckv_p, l3_kpe_p, l3_ckv_s, l3_kpe_s = _mla_layer(
        xp, xs, l3_norm, l3_q_norm, l3_kv_norm, w3, cache_l3_ckv, cache_l3_kpe, tabs_p, tabs_s,
        batch=batch, seq=seq, fin=final_norm, tag="l3")

    r3 = lambda a, n: a.reshape(n, -1, a.shape[-1])
    return (yp.reshape(batch, seq, d), ys.reshape(nb, new, d),
            r3(l0_ckv_p, batch), r3(l0_kpe_p, batch), r3(l0_ckv_s, nb), r3(l0_kpe_s, nb),
            l1_conv_p, l1_conv_s,
            l2_conv_p, l2_ssm_p, l2_conv_s, l2_ssm_s,
            r3(l3_ckv_p, batch), r3(l3_kpe_p, batch), r3(l3_ckv_s, nb), r3(l3_kpe_s, nb))
```

```python
import functools

import jax
import jax.numpy as jnp
from jax import lax
from jax.experimental import pallas as pl
from jax.experimental.pallas import tpu as pltpu

F32 = jnp.float32
BF16 = jnp.bfloat16

EPS = 1e-6
ROPE_THETA = 10000.0
CHUNK = 64
HEADS = 16
NOPE = 128
ROPE = 64
HEAD_PAD = 256
MLA_SCALE = (NOPE + ROPE) ** -0.5
LOG2E = 1.4426950408889634
LORA = 512
CONV_K = 31
CONV_HIST = 32
SSM_GROUPS = 8
SSM_GROUP_HEADS = 8
SSM_HEAD_DIM = 64
SSM_GROUP_W = SSM_GROUP_HEADS * SSM_HEAD_DIM
D_STATE = 128
SSM_CONV_K = 4
SSM_HIST = 8
LANE = 128

VMEM_LIMIT_BYTES = 56 * 1024 * 1024


def _cparams(*sem):
    return pltpu.CompilerParams(dimension_semantics=sem, vmem_limit_bytes=VMEM_LIMIT_BYTES)


def _resident(shape, index_map):
    return pl.BlockSpec(shape, index_map, pipeline_mode=pl.Buffered(1))


def _sigmoid(x):
    return 0.5 * jnp.tanh(0.5 * x) + 0.5


def _silu(x):
    h = 0.5 * x
    return h + h * jnp.tanh(h)


def _split3(x):
    hi = x.astype(BF16)
    r = x - hi.astype(F32)
    mid = r.astype(BF16)
    return hi, mid, (r - mid.astype(F32)).astype(BF16)


def _dot_onehot_lhs(sel, x):
    hi, mid, lo = _split3(x)
    return _dot(sel, hi) + (_dot(sel, mid) + _dot(sel, lo))


def _dot_onehot_rhs(x, sel):
    hi, mid, lo = _split3(x)
    return _dot(hi, sel) + (_dot(mid, sel) + _dot(lo, sel))


def _rms(x, g):
    return (x * lax.rsqrt(jnp.mean(x * x, axis=-1, keepdims=True) + EPS)) * g


def _dot(a, b):
    return jnp.dot(a, b, preferred_element_type=F32)


def _dot_nt(a, b):
    return lax.dot_general(a, b, (((1,), (1,)), ((), ())), preferred_element_type=F32)


def _norm_matmul_kernel(x_ref, g_ref, w_ref, o_ref, *, col_chunk):
    xn = _rms(x_ref[...], g_ref[...]).astype(BF16)
    for c in range(0, o_ref.shape[1], col_chunk):
        o_ref[:, c:c + col_chunk] = _dot(xn, w_ref[:, c:c + col_chunk]).astype(o_ref.dtype)


def norm_matmul(x, g, w, *, tm, tn, out_dtype, name, n_out=None):
    m, k = x.shape
    n = w.shape[1] if n_out is None else n_out
    col_chunk = 256 if tn % 256 == 0 else LANE
    return pl.pallas_call(
        functools.partial(_norm_matmul_kernel, col_chunk=col_chunk),
        grid=(n // tn, m // tm),
        in_specs=[pl.BlockSpec((tm, k), lambda j, i: (i, 0)),
                  _resident((1, k), lambda j, i: (0, 0)),
                  _resident((k, tn), lambda j, i: (0, j))],
        out_specs=pl.BlockSpec((tm, tn), lambda j, i: (i, j)),
        out_shape=jax.ShapeDtypeStruct((m, n), out_dtype),
        compiler_params=_cparams("arbitrary", "arbitrary"),
        name=name,
    )(x, g.reshape(1, k), w)


def _out_proj_kernel(*refs, gated, layer_norm, final):
    refs = list(refs)
    a_ref = refs.pop(0)
    gate_ref = refs.pop(0) if gated else None
    lng_ref, lnb_ref = (refs.pop(0), refs.pop(0)) if layer_norm else (None, None)
    w_ref, res_ref = refs.pop(0), refs.pop(0)
    fin_ref = refs.pop(0) if final else None
    o_ref = refs.pop(0)
    a = a_ref[...]
    if layer_norm:
        cen = a - jnp.mean(a, axis=-1, keepdims=True)
        var = jnp.mean(cen * cen, axis=-1, keepdims=True)
        a = _silu((cen * lax.rsqrt(var + EPS)) * lng_ref[...] + lnb_ref[...])
    if gated:
        a = a.astype(F32) * _silu(gate_ref[...].astype(F32))
    y = res_ref[...] + _dot(a.astype(BF16), w_ref[...])
    if final:
        y = _rms(y, fin_ref[...])
    o_ref[...] = y


def out_proj(a, gate_src, w, res, fin, *, tm, name, gate_block=0, ln=None):
    m, k = a.shape
    n = w.shape[1]
    gated, final, layer_norm = gate_src is not None, fin is not None, ln is not None
    in_specs = [pl.BlockSpec((tm, k), lambda i: (i, 0))]
    args = [a]
    if gated:
        in_specs.append(pl.BlockSpec((tm, k), lambda i: (i, gate_block)))
        args.append(gate_src)
    if layer_norm:
        in_specs += [_resident((1, k), lambda i: (0, 0)), _resident((1, k), lambda i: (0, 0))]
        args += [ln[0].reshape(1, k), ln[1].reshape(1, k)]
    in_specs += [_resident((k, n), lambda i: (0, 0)), pl.BlockSpec((tm, n), lambda i: (i, 0))]
    args += [w, res]
    if final:
        in_specs.append(_resident((1, n), lambda i: (0, 0)))
        args.append(fin.reshape(1, n))
    return pl.pallas_call(
        functools.partial(_out_proj_kernel, gated=gated, layer_norm=layer_norm, final=final),
        grid=(m // tm,),
        in_specs=in_specs,
        out_specs=pl.BlockSpec((tm, n), lambda i: (i, 0)),
        out_shape=jax.ShapeDtypeStruct((m, n), F32),
        compiler_params=_cparams("parallel"),
        name=name,
    )(*args)


MLA_GATE_W = HEADS * NOPE
MLA_PROJ_W = MLA_GATE_W + 2 * LORA + 2 * LANE


def _mla_mid_kernel(*refs, with_kv):
    (qa_ref, ckv_ref, kpe_ref, cos_ref, sin_ref, qn_ref, kvn_ref,
     wqn_ref, wqr_ref, wqr2_ref) = refs[:10]
    refs = refs[10:]
    if with_kv:
        wk_ref, wv_ref, q_out, ckv_out, kpe_out, k_out, v_out = refs
    else:
        q_out, ckv_out, kpe_out = refs
    cosp, sinp = cos_ref[...], sin_ref[...]
    q_scale = MLA_SCALE * LOG2E if with_kv else MLA_SCALE

    qa = _rms(qa_ref[...].astype(F32), qn_ref[...]).astype(BF16)
    qn = _dot(qa, wqn_ref[...]) * q_scale
    qr = _dot(qa, wqr_ref[...])
    qr2 = _dot(qa, wqr2_ref[...])
    for h in range(HEADS):
        lo = h * LANE
        q_out[:, 2 * lo:2 * lo + LANE] = qn[:, lo:lo + LANE].astype(BF16)
        rot = (qr[:, lo:lo + LANE] * cosp + qr2[:, lo:lo + LANE] * sinp) * q_scale
        q_out[:, 2 * lo + LANE:2 * lo + 2 * LANE] = rot.astype(BF16)

    ckv = _rms(ckv_ref[...].astype(F32), kvn_ref[...])
    ckv_out[...] = ckv
    kp = kpe_ref[...].astype(F32)
    kpe = kp[:, :LANE] * cosp + kp[:, LANE:] * sinp
    kpe_out[...] = kpe[:, :ROPE]
    if with_kv:
        ckv_b = ckv.astype(BF16)
        kpe_b = kpe.astype(BF16)
        kn = _dot(ckv_b, wk_ref[...])
        for h in range(HEADS):
            lo = h * LANE
            k_out[:, 2 * lo:2 * lo + LANE] = kn[:, lo:lo + LANE].astype(BF16)
            k_out[:, 2 * lo + LANE:2 * lo + 2 * LANE] = kpe_b
        v_out[0] = _dot_nt(wv_ref[...], ckv_b).astype(BF16)


def mla_mid(proj, cos_t, sin_t, q_norm, kv_norm, wqn, wqr, wqr2, wk, wv, *, tm, with_kv, name):
    m = proj.shape[0]
    nt = cos_t.shape[0] // tm
    hw = HEADS * LANE
    in_specs = [
        pl.BlockSpec((tm, LORA), lambda i: (i, MLA_GATE_W // LORA)),
        pl.BlockSpec((tm, LORA), lambda i: (i, MLA_GATE_W // LORA + 1)),
        pl.BlockSpec((tm, 2 * LANE), lambda i: (i, (MLA_GATE_W + 2 * LORA) // (2 * LANE))),
        pl.BlockSpec((tm, LANE), lambda i: (i % nt, 0)),
        pl.BlockSpec((tm, LANE), lambda i: (i % nt, 0)),
        _resident((1, LORA), lambda i: (0, 0)),
        _resident((1, LORA), lambda i: (0, 0)),
        _resident((LORA, hw), lambda i: (0, 0)),
        _resident((LORA, hw), lambda i: (0, 0)),
        _resident((LORA, hw), lambda i: (0, 0)),
    ]
    args = [proj, proj, proj, cos_t, sin_t, q_norm.reshape(1, LORA), kv_norm.reshape(1, LORA), wqn, wqr, wqr2]
    out_specs = [pl.BlockSpec((tm, HEADS * HEAD_PAD), lambda i: (i, 0)),
                 pl.BlockSpec((tm, LORA), lambda i: (i, 0)),
                 pl.BlockSpec((tm, ROPE), lambda i: (i, 0))]
    out_shape = [jax.ShapeDtypeStruct((m, HEADS * HEAD_PAD), BF16),
                 jax.ShapeDtypeStruct((m, LORA), F32),
                 jax.ShapeDtypeStruct((m, ROPE), F32)]
    if with_kv:
        assert tm == VT_COLS
        in_specs += [_resident((LORA, hw), lambda i: (0, 0)), _resident((hw, LORA), lambda i: (0, 0))]
        args += [wk, wv]
        out_specs += [pl.BlockSpec((tm, HEADS * HEAD_PAD), lambda i: (i, 0)),
                      pl.BlockSpec((1, hw, tm), lambda i: (i, 0, 0))]
        out_shape += [jax.ShapeDtypeStruct((m, HEADS * HEAD_PAD), BF16),
                      jax.ShapeDtypeStruct((m // tm, hw, tm), BF16)]
    return pl.pallas_call(
        functools.partial(_mla_mid_kernel, with_kv=with_kv),
        grid=(m // tm,),
        in_specs=in_specs, out_specs=out_specs, out_shape=out_shape,
        compiler_params=_cparams("parallel"),
        name=name,
    )(*args)


NEG_BIG = -1e30


FLASH_HEADS = 2


VT_COLS = 256


def _flash_kernel(q_ref, k_ref, vt_ref, o_ref, *, tile):
    qi = pl.program_id(2)
    slabs = tile // VT_COLS

    def step(ki, carry, masked):
        start = pl.multiple_of(ki * tile, tile)
        out = []
        for h in range(FLASH_HEADS):
            m_prev, l_prev, acc = carry[h]
            q = q_ref[:, h * HEAD_PAD:(h + 1) * HEAD_PAD]
            st = _dot_nt(k_ref[pl.ds(start, tile), h * HEAD_PAD:(h + 1) * HEAD_PAD], q)
            if masked:
                kc = lax.broadcasted_iota(jnp.int32, st.shape, 0) // CHUNK
                qc = lax.broadcasted_iota(jnp.int32, st.shape, 1) // CHUNK
                st = jnp.where(kc <= qc, st, NEG_BIG)
            m_new = jnp.maximum(m_prev, jnp.max(st, axis=0, keepdims=True))
            alpha = jnp.exp2(m_prev - m_new)
            p = jnp.exp2(st - m_new)
            l_new = alpha * l_prev + jnp.sum(p, axis=0, keepdims=True)
            p = p.astype(BF16)
            pv = None
            for j in range(slabs):
                vt = vt_ref[ki * slabs + j, h * NOPE:(h + 1) * NOPE, :]
                part = _dot(vt, p[j * VT_COLS:(j + 1) * VT_COLS, :])
                pv = part if pv is None else pv + part
            out.append((m_new, l_new, alpha * acc + pv))
        return tuple(out)

    init = tuple((jnp.full((1, tile), NEG_BIG, F32), jnp.zeros((1, tile), F32), jnp.zeros((NOPE, tile), F32))
                 for _ in range(FLASH_HEADS))
    carry = lax.fori_loop(0, qi, lambda ki, c: step(ki, c, False), init)
    carry = step(qi, carry, True)
    for h in range(FLASH_HEADS):
        _, l_fin, acc = carry[h]
        o_ref[:, h * NOPE:(h + 1) * NOPE] = (acc / l_fin).T.astype(o_ref.dtype)


def flash_attention(q, k, vt, *, batch, seq, tile, name):
    nq = seq // tile
    hq, hv = FLASH_HEADS * HEAD_PAD, FLASH_HEADS * NOPE
    return pl.pallas_call(
        functools.partial(_flash_kernel, tile=tile),
        grid=(batch, HEADS // FLASH_HEADS, nq),
        in_specs=[pl.BlockSpec((tile, hq), lambda b, h, i: (b * nq + i, h)),
                  pl.BlockSpec((seq, hq), lambda b, h, i: (b, h)),
                  pl.BlockSpec((seq // VT_COLS, hv, VT_COLS), lambda b, h, i: (b, h, 0))],
        out_specs=pl.BlockSpec((tile, hv), lambda b, h, i: (b * nq + i, h)),
        out_shape=jax.ShapeDtypeStruct((batch * seq, HEADS * NOPE), BF16),
        compiler_params=_cparams("parallel", "parallel", "arbitrary"),
        name=name,
    )(q, k, vt)


def _sample_attn_kernel(q_ref, cnew_ref, pnew_ref, ccache_ref, pcache_ref, wk_ref, wv_ref, o_ref,
                        kc_scr, kp_scr, *, past, new, pad):
    total = past + pad
    kc_scr[0:past, :] = ccache_ref[0].astype(BF16)
    kp_scr[0:past, :] = pcache_ref[0].astype(BF16)
    kc_scr[past:total, :] = jnp.zeros((pad, LORA), BF16)
    kp_scr[past:total, :] = jnp.zeros((pad, ROPE), BF16)
    kc_scr[past:past + new, :] = cnew_ref[...].astype(BF16)
    kp_scr[past:past + new, :] = pnew_ref[...].astype(BF16)

    q = q_ref[...]
    qlat, qpe = [], []
    for h in range(HEADS):
        lo = h * HEAD_PAD
        qlat.append(_dot_nt(q[:, lo:lo + NOPE], wk_ref[:, h * NOPE:(h + 1) * NOPE]))
        qpe.append(q[:, lo + NOPE:lo + NOPE + ROPE])
    qlat = jnp.concatenate(qlat, axis=0).astype(BF16)
    qpe = jnp.concatenate(qpe, axis=0)
    kc = kc_scr[...]
    s = _dot_nt(qlat, kc) + _dot_nt(qpe, kp_scr[...])
    col = lax.broadcasted_iota(jnp.int32, s.shape, 1)
    s = jnp.where(col < past + new, s, NEG_BIG)
    m = jnp.max(s, axis=-1, keepdims=True)
    p = jnp.exp(s - m)
    l = jnp.sum(p, axis=-1, keepdims=True)
    olat = (_dot(p.astype(BF16), kc) / l).astype(BF16)
    for h in range(HEADS):
        o_ref[:, h * NOPE:(h + 1) * NOPE] = _dot(
            olat[h * new:(h + 1) * new, :], wv_ref[:, h * NOPE:(h + 1) * NOPE]).astype(o_ref.dtype)


def sample_attention(q, ckv_new, kpe_new, ckv_cache, kpe_cache, wk, wv, *, name):
    nb, past, _ = ckv_cache.shape
    new = q.shape[0] // nb
    pad = LANE
    hw = HEADS * NOPE
    return pl.pallas_call(
        functools.partial(_sample_attn_kernel, past=past, new=new, pad=pad),
        grid=(nb,),
        in_specs=[pl.BlockSpec((new, HEADS * HEAD_PAD), lambda b: (b, 0)),
                  pl.BlockSpec((new, LORA), lambda b: (b, 0)),
                  pl.BlockSpec((new, ROPE), lambda b: (b, 0)),
                  pl.BlockSpec((1, past, LORA), lambda b: (b, 0, 0)),
                  pl.BlockSpec((1, past, ROPE), lambda b: (b, 0, 0)),
                  _resident((LORA, hw), lambda b: (0, 0)),
                  _resident((LORA, hw), lambda b: (0, 0))],
        out_specs=pl.BlockSpec((new, hw), lambda b: (b, 0)),
        out_shape=jax.ShapeDtypeStruct((nb * new, hw), BF16),
        scratch_shapes=[pltpu.VMEM((past + pad, LORA), BF16), pltpu.VMEM((past + pad, ROPE), BF16)],
        compiler_params=_cparams("arbitrary"),
        name=name,
    )(q, ckv_new, kpe_new, ckv_cache, kpe_cache, wk, wv)


SUBLANES = 8
CONV_COLS = 512


def _conv_kernel(val_ref, glu_ref, hist_ref, dw_ref, dwb_ref, y_ref, tail_ref, ubuf, shifted, *, tt):
    i = pl.program_id(2)

    @pl.when(i == 0)
    def _():
        ubuf[0:CONV_HIST, :] = hist_ref[0]

    ubuf[CONV_HIST:CONV_HIST + tt, :] = val_ref[...].astype(F32) * _sigmoid(glu_ref[...].astype(F32))
    span = tt + CONV_HIST - SUBLANES
    for r in range(1, SUBLANES):
        shifted[r - 1] = ubuf[r:r + span, :]
    first = CONV_HIST - (CONV_K - 1)
    for c in range(0, ubuf.shape[1], LANE):
        acc = None
        for k in range(CONV_K):
            q, r = divmod(first + k, SUBLANES)
            rows = pl.ds(q * SUBLANES, tt)
            src = ubuf[rows, c:c + LANE] if r == 0 else shifted[r - 1, rows, c:c + LANE]
            term = src * dw_ref[k:k + 1, c:c + LANE]
            acc = term if acc is None else acc + term
        y_ref[:, c:c + LANE] = acc + dwb_ref[:, c:c + LANE]
    tail = ubuf[tt:tt + CONV_HIST, :]
    tail_ref[0] = tail
    ubuf[0:CONV_HIST, :] = tail


def conv_mixer(proj, hist, dw_w, dw_b, *, batch, seq, tt, name):
    d = dw_w.shape[1]
    nt, cw = seq // tt, CONV_COLS
    nc = d // cw
    return pl.pallas_call(
        functools.partial(_conv_kernel, tt=tt),
        grid=(batch, nc, nt),
        in_specs=[pl.BlockSpec((tt, cw), lambda b, j, i: (b * nt + i, j)),
                  pl.BlockSpec((tt, cw), lambda b, j, i: (b * nt + i, nc + j)),
                  pl.BlockSpec((1, CONV_HIST, cw), lambda b, j, i: (b, 0, j)),
                  pl.BlockSpec((CONV_K, cw), lambda b, j, i: (0, j)),
                  pl.BlockSpec((1, cw), lambda b, j, i: (0, j))],
        out_specs=[pl.BlockSpec((tt, cw), lambda b, j, i: (b * nt + i, j)),
                   pl.BlockSpec((1, CONV_HIST, cw), lambda b, j, i: (b, 0, j))],
        out_shape=[jax.ShapeDtypeStruct((batch * seq, d), F32),
                   jax.ShapeDtypeStruct((batch, CONV_HIST, d), F32)],
        scratch_shapes=[pltpu.VMEM((CONV_HIST + tt, cw), F32),
                        pltpu.VMEM((SUBLANES - 1, tt + CONV_HIST - SUBLANES, cw), F32)],
        compiler_params=_cparams("parallel", "parallel", "arbitrary"),
        name=name,
    )(proj, proj, hist, dw_w, dw_b.reshape(1, d))


def _softplus(x):
    return jnp.maximum(x, 0.0) + jnp.log1p(jnp.exp(-jnp.abs(x)))


SSD_GROUPS_PER_STEP = 1


def _ssd_kernel(z_ref, x_ref, b_ref, c_ref, dt_ref, hx_ref, hb_ref, hc_ref, st0_ref,
                cwx_ref, cwb_ref, cwc_ref, cbx_ref, cbb_ref, cbc_ref,
                dtb_ref, alog_ref, dskip_ref, gn_ref, expb_ref,
                y_ref, tx_ref, tb_ref, tc_ref, st_ref,
                xbuf, bbuf, cbuf, state, *, L, valid):
    gw, ns = SSM_GROUP_W, D_STATE
    for gi in range(SSD_GROUPS_PER_STEP):
        ch, sn, hd, one = pl.ds(gi * gw, gw), pl.ds(gi * ns, ns), pl.ds(gi * LANE, LANE), pl.ds(gi, 1)
        _ssd_group(
            z_ref.at[:, ch], x_ref.at[:, ch], b_ref.at[:, sn], c_ref.at[:, sn], dt_ref.at[:, hd],
            hx_ref.at[:, :, ch], hb_ref.at[:, :, sn], hc_ref.at[:, :, sn], st0_ref.at[:, ch, :],
            cwx_ref.at[one], cwb_ref.at[one], cwc_ref.at[one], cbx_ref.at[one], cbb_ref.at[one], cbc_ref.at[one],
            dtb_ref.at[one], alog_ref.at[one], dskip_ref.at[one], gn_ref.at[one], expb_ref,
            y_ref.at[:, ch], tx_ref.at[:, :, ch], tb_ref.at[:, :, sn], tc_ref.at[:, :, sn], st_ref.at[:, ch, :],
            xbuf.at[:, ch], bbuf.at[:, sn], cbuf.at[:, sn], state.at[gi], L=L, valid=valid)


def _ssd_group(z_ref, x_ref, b_ref, c_ref, dt_ref, hx_ref, hb_ref, hc_ref, st0_ref,
               cwx_ref, cwb_ref, cwc_ref, cbx_ref, cbb_ref, cbc_ref,
               dtb_ref, alog_ref, dskip_ref, gn_ref, expb_ref,
               y_ref, tx_ref, tb_ref, tc_ref, st_ref,
               xbuf, bbuf, cbuf, state, *, L, valid):
    ci = pl.program_id(2)

    @pl.when(ci == 0)
    def _():
        xbuf[0:SSM_HIST, :] = hx_ref[0]
        bbuf[0:SSM_HIST, :] = hb_ref[0]
        cbuf[0:SSM_HIST, :] = hc_ref[0]
        state[...] = st0_ref[0].T

    def short_conv(buf, new_ref, w_ref, bias_ref, tail_ref):
        buf[SSM_HIST:SSM_HIST + L, :] = new_ref[...].astype(F32)
        off = SSM_HIST - (SSM_CONV_K - 1)
        acc = buf[off:off + L, :] * w_ref[0, 0:1, :]
        for k in range(1, SSM_CONV_K):
            acc = acc + buf[off + k:off + k + L, :] * w_ref[0, k:k + 1, :]
        tail = buf[valid:valid + SSM_HIST, :]
        tail_ref[0] = tail
        buf[0:SSM_HIST, :] = tail
        return _silu(acc + bias_ref[0])

    x = short_conv(xbuf, x_ref, cwx_ref, cbx_ref, tx_ref)
    bm = short_conv(bbuf, b_ref, cwb_ref, cbb_ref, tb_ref)
    cm = short_conv(cbuf, c_ref, cwc_ref, cbc_ref, tc_ref)

    row = lax.broadcasted_iota(jnp.int32, (L, L), 0)
    col = lax.broadcasted_iota(jnp.int32, (L, L), 1)
    causal = col <= row
    dt = _softplus(dt_ref[...] + dtb_ref[0])
    if valid < L:
        dt = jnp.where(lax.broadcasted_iota(jnp.int32, dt.shape, 0) < valid, dt, 0.0)
    da = dt * (-jnp.exp(alog_ref[0]))
    acs = _dot_onehot_lhs(causal.astype(BF16), da)
    acs_last = acs[L - 1:L, :]
    eacs = jnp.exp(acs)
    dte = jnp.exp(acs_last - acs) * dt
    acs_t = acs.T
    dt_t = dt.T

    eacs_x = _dot(eacs.astype(BF16), expb_ref[...])
    dte_x = _dot(dte.astype(BF16), expb_ref[...])
    cdec_x = _dot_onehot_rhs(jnp.broadcast_to(jnp.exp(acs_last), (SUBLANES, LANE)), expb_ref[...])[0:1, :]

    x_b = x.astype(BF16)
    bm_b = bm.astype(BF16)
    cm_b = cm.astype(BF16)
    cb = _dot_nt(cm_b, bm_b)
    st = state[...]
    y_off = _dot(cm_b, st.astype(BF16)) * eacs_x

    low_half = lax.broadcasted_iota(jnp.int32, (L, LANE), 1) < SSM_HEAD_DIM
    pairs = []
    for pr in range(SSM_GROUP_HEADS // 2):
        xp = x_b[:, pr * LANE:(pr + 1) * LANE]
        ys = []
        for e in (2 * pr, 2 * pr + 1):
            seg = jnp.where(causal, jnp.exp(acs[:, e:e + 1] - acs_t[e:e + 1, :]), 0.0)
            mm = (cb * seg * dt_t[e:e + 1, :]).astype(BF16)
            ys.append(_dot(mm, xp))
        pairs.append(jnp.where(low_half, ys[0], ys[1]))
    y = jnp.concatenate(pairs, axis=1) + y_off + dskip_ref[0] * x

    state[...] = st * cdec_x + _dot(bm.T.astype(BF16), (x * dte_x).astype(BF16))

    yz = y * _silu(z_ref[...].astype(F32))
    yn = (yz * lax.rsqrt(jnp.mean(yz * yz, axis=-1, keepdims=True) + EPS)) * gn_ref[0]
    y_ref[...] = yn.astype(y_ref.dtype)
    st_ref[0] = state[...].T


def ssd_mixer(proj, dtp, hist_x, hist_b, hist_c, st0, prm, *, batch, nchunks, L, valid, name):
    gp = SSD_GROUPS_PER_STEP
    gw, ns = SSM_GROUP_W, D_STATE
    sw, sn, sl = gp * gw, gp * ns, gp * LANE
    xb = (SSM_GROUPS * gw) // sw
    bb = (2 * SSM_GROUPS * gw) // sn
    cb = bb + SSM_GROUPS // gp
    rows = batch * nchunks * L
    rmap = lambda off: (lambda b, g, c: (b * nchunks + c, off + g))
    hmap = lambda b, g, c: (b, 0, g)
    pmap = lambda b, g, c: (g, 0, 0)
    in_specs = [
        pl.BlockSpec((L, sw), rmap(0)), pl.BlockSpec((L, sw), rmap(xb)),
        pl.BlockSpec((L, sn), rmap(bb)), pl.BlockSpec((L, sn), rmap(cb)),
        pl.BlockSpec((L, sl), rmap(0)),
        pl.BlockSpec((1, SSM_HIST, sw), hmap), pl.BlockSpec((1, SSM_HIST, sn), hmap),
        pl.BlockSpec((1, SSM_HIST, sn), hmap),
        pl.BlockSpec((1, sw, ns), lambda b, g, c: (b, g, 0)),
        pl.BlockSpec((gp, SSM_CONV_K, gw), pmap), pl.BlockSpec((gp, SSM_CONV_K, ns), pmap),
        pl.BlockSpec((gp, SSM_CONV_K, ns), pmap),
        pl.BlockSpec((gp, 1, gw), pmap), pl.BlockSpec((gp, 1, ns), pmap), pl.BlockSpec((gp, 1, ns), pmap),
        pl.BlockSpec((gp, 1, LANE), pmap), pl.BlockSpec((gp, 1, LANE), pmap),
        pl.BlockSpec((gp, 1, gw), pmap), pl.BlockSpec((gp, 1, gw), pmap),
        _resident((LANE, gw), lambda b, g, c: (0, 0)),
    ]
    expand = (jnp.arange(gw)[None, :] // SSM_HEAD_DIM) == jnp.arange(LANE)[:, None]
    out_specs = [
        pl.BlockSpec((L, sw), rmap(0)),
        pl.BlockSpec((1, SSM_HIST, sw), hmap), pl.BlockSpec((1, SSM_HIST, sn), hmap),
        pl.BlockSpec((1, SSM_HIST, sn), hmap),
        pl.BlockSpec((1, sw, ns), lambda b, g, c: (b, g, 0)),
    ]
    out_shape = [
        jax.ShapeDtypeStruct((rows, SSM_GROUPS * gw), BF16),
        jax.ShapeDtypeStruct((batch, SSM_HIST, SSM_GROUPS * gw), F32),
        jax.ShapeDtypeStruct((batch, SSM_HIST, SSM_GROUPS * ns), F32),
        jax.ShapeDtypeStruct((batch, SSM_HIST, SSM_GROUPS * ns), F32),
        jax.ShapeDtypeStruct((batch, SSM_GROUPS * gw, ns), F32),
    ]
    return pl.pallas_call(
        functools.partial(_ssd_kernel, L=L, valid=valid),
        grid=(batch, SSM_GROUPS // gp, nchunks),
        in_specs=in_specs, out_specs=out_specs, out_shape=out_shape,
        scratch_shapes=[pltpu.VMEM((SSM_HIST + L, sw), F32), pltpu.VMEM((SSM_HIST + L, sn), F32),
                        pltpu.VMEM((SSM_HIST + L, sn), F32), pltpu.VMEM((gp, ns, gw), F32)],
        compiler_params=_cparams("parallel", "parallel", "arbitrary"),
        name=name,
    )(proj, proj, proj, proj, dtp, hist_x, hist_b, hist_c, st0,
      prm["cwx"], prm["cwb"], prm["cwc"], prm["cbx"], prm["cbb"], prm["cbc"],
      prm["dtb"], prm["alog"], prm["dskip"], prm["gn"], expand.astype(BF16))


def _rope_tables(pos):
    half = ROPE // 2
    freqs = ROPE_THETA ** (-jnp.arange(half, dtype=F32) / half)
    ang = pos.astype(F32)[:, None] * freqs[None, :]
    zeros = jnp.zeros((pos.shape[0], LANE - ROPE), F32)
    cos, sin = jnp.cos(ang), jnp.sin(ang)
    return jnp.concatenate([cos, cos, zeros], axis=1), jnp.concatenate([sin, sin, zeros], axis=1)


def _rot_cols(w):
    half = ROPE // 2
    return jnp.concatenate([-w[..., half:], w[..., :half]], axis=-1)


def _mla_weights(w_in, w_qb, w_kvb, w_out):
    d = w_in.shape[0]
    qa, ckv = w_in[:, :LORA], w_in[:, LORA:2 * LORA]
    kpe, gate = w_in[:, 2 * LORA:2 * LORA + ROPE], w_in[:, 2 * LORA + ROPE:]
    z = jnp.zeros((d, LANE - ROPE), F32)
    w0 = jnp.concatenate([gate, qa, ckv, kpe, z, _rot_cols(kpe), z], axis=1).astype(BF16)
    hw = HEADS * NOPE
    r = w_qb[:, :, NOPE:]
    zr = jnp.zeros((LORA, HEADS, LANE - ROPE), F32)
    return dict(
        w0=w0,
        wqn=w_qb[:, :, :NOPE].reshape(LORA, hw).astype(BF16),
        wqr=jnp.concatenate([r, zr], axis=-1).reshape(LORA, hw).astype(BF16),
        wqr2=jnp.concatenate([_rot_cols(r), zr], axis=-1).reshape(LORA, hw).astype(BF16),
        wk=w_kvb[:, :, :NOPE].reshape(LORA, hw).astype(BF16),
        wv=w_kvb[:, :, NOPE:].reshape(LORA, hw).astype(BF16),
        wo=w_out.astype(BF16),
    )


def _mla_layer(xp, xs, norm_g, q_norm, kv_norm, wts, ckv_cache, kpe_cache, tabs_p, tabs_s,
               *, batch, seq, fin, tag):
    mid = functools.partial(mla_mid, q_norm=q_norm, kv_norm=kv_norm, wqn=wts["wqn"], wqr=wts["wqr"],
                            wqr2=wts["wqr2"], wk=wts["wk"])
    proj_p = norm_matmul(xp, norm_g, wts["w0"], tm=512, tn=MLA_PROJ_W, out_dtype=BF16, name=tag + "_inproj_p")
    q, ckv_p, kpe_p, k, vt = mid(proj_p, *tabs_p, wv=wts["wv"].T, tm=VT_COLS, with_kv=True, name=tag + "_mid_p")
    o = flash_attention(q, k, vt, batch=batch, seq=seq, tile=min(seq, 1024), name=tag + "_flash")
    xp = out_proj(o, proj_p, wts["wo"], xp, fin, tm=256, name=tag + "_outproj_p")

    ms = xs.shape[0]
    proj_s = norm_matmul(xs, norm_g, wts["w0"], tm=ms, tn=MLA_PROJ_W, out_dtype=BF16, name=tag + "_inproj_s")
    qs, ckv_s, kpe_s = mid(proj_s, *tabs_s, wv=None, tm=ms, with_kv=False, name=tag + "_mid_s")
    os_ = sample_attention(qs, ckv_s, kpe_s, ckv_cache, kpe_cache, wts["wk"], wts["wv"], name=tag + "_attn_s")
    xs = out_proj(os_, proj_s, wts["wo"], xs, fin, tm=ms, name=tag + "_outproj_s")
    return xp, xs, ckv_p, kpe_p, ckv_s, kpe_s


def _ssd_layer(xp, xs, l2_norm, l2_w_in, l2_conv_w, l2_conv_b, l2_dt_bias, l2_a_log, l2_d_skip, l2_gnorm, l2_w_out,
               state_l2_conv, state_l2_ssm, *, batch, seq, nb, new, chunk_p=256, chunk_s=None):
    d = xp.shape[1]
    ms = nb * new
    inner = SSM_GROUPS * SSM_GROUP_W
    bc = SSM_GROUPS * D_STATE
    main_w = 2 * inner + 2 * bc
    w2 = l2_w_in.astype(BF16)
    w2dt = jnp.pad(l2_w_in[:, main_w:].reshape(d, SSM_GROUPS, SSM_GROUP_HEADS),
                   ((0, 0), (0, 0), (0, LANE - SSM_GROUP_HEADS))).reshape(d, SSM_GROUPS * LANE).astype(BF16)
    w2o = l2_w_out.astype(BF16)
    per_head = lambda v: jnp.pad(v.reshape(SSM_GROUPS, 1, SSM_GROUP_HEADS),
                                 ((0, 0), (0, 0), (0, LANE - SSM_GROUP_HEADS)))
    cw = l2_conv_w
    grp = lambda w, width: w.reshape(w.shape[0], SSM_GROUPS, width).transpose(1, 0, 2)
    prm = dict(
        cwx=grp(cw[:, :inner], SSM_GROUP_W), cwb=grp(cw[:, inner:inner + bc], D_STATE),
        cwc=grp(cw[:, inner + bc:], D_STATE),
        cbx=grp(l2_conv_b[None, :inner], SSM_GROUP_W), cbb=grp(l2_conv_b[None, inner:inner + bc], D_STATE),
        cbc=grp(l2_conv_b[None, inner + bc:], D_STATE),
        dtb=per_head(l2_dt_bias), alog=per_head(l2_a_log),
        dskip=jnp.repeat(l2_d_skip, SSM_HEAD_DIM).reshape(SSM_GROUPS, 1, SSM_GROUP_W),
        gn=l2_gnorm.reshape(SSM_GROUPS, 1, SSM_GROUP_W),
    )
    hist_rows = SSM_CONV_K - 1

    def run_ssd(proj, dtp, hist, st0, nbatch, nchunks, L, valid, name):
        hist = jnp.pad(hist, ((0, 0), (SSM_HIST - hist_rows, 0), (0, 0)))
        y, tx, tb, tc, st = ssd_mixer(
            proj, dtp, hist[:, :, :inner], hist[:, :, inner:inner + bc], hist[:, :, inner + bc:],
            st0.reshape(nbatch, inner, D_STATE), prm, batch=nbatch, nchunks=nchunks, L=L, valid=valid, name=name)
        tail = jnp.concatenate([tx, tb, tc], axis=-1)[:, SSM_HIST - hist_rows:]
        return y, tail, st.reshape(nbatch, SSM_GROUPS * SSM_GROUP_HEADS, SSM_HEAD_DIM, D_STATE)

    proj_p = norm_matmul(xp, l2_norm, w2, tm=256, tn=main_w // 2, n_out=main_w, out_dtype=BF16, name="l2_inproj_p")
    dt_p = norm_matmul(xp, l2_norm, w2dt, tm=512, tn=SSM_GROUPS * LANE, out_dtype=F32, name="l2_dtproj_p")
    y_p, l2_conv_p, l2_ssm_p = run_ssd(
        proj_p, dt_p, jnp.zeros((batch, hist_rows, inner + 2 * bc), F32),
        jnp.zeros((batch, inner, D_STATE), F32), batch, seq // chunk_p, chunk_p, chunk_p, "l2_ssd_p")
    xp = out_proj(y_p, None, w2o, xp, None, tm=256, name="l2_outproj_p")

    proj_s = norm_matmul(xs, l2_norm, w2, tm=ms, tn=main_w // 2, n_out=main_w, out_dtype=BF16, name="l2_inproj_s")
    dt_s = norm_matmul(xs, l2_norm, w2dt, tm=ms, tn=SSM_GROUPS * LANE, out_dtype=F32, name="l2_dtproj_s")
    L = chunk_s if chunk_s is not None else -(-new // SUBLANES) * SUBLANES
    pad_rows = lambda a: jnp.pad(a.reshape(nb, new, a.shape[1]), ((0, 0), (0, L - new), (0, 0))).reshape(nb * L, a.shape[1])
    y_s, l2_conv_s, l2_ssm_s = run_ssd(pad_rows(proj_s), pad_rows(dt_s), state_l2_conv, state_l2_ssm,
                                       nb, 1, L, new, "l2_ssd_s")
    y_s = y_s.reshape(nb, L, inner)[:, :new].reshape(ms, inner)
    xs = out_proj(y_s, None, w2o, xs, None, tm=ms, name="l2_outproj_s")

    return xp, xs, l2_conv_p, l2_ssm_p, l2_conv_s, l2_ssm_s


def kernel(x_prompt, x_sample, cache_l0_ckv, cache_l0_kpe, state_l1_conv, state_l2_conv, state_l2_ssm, cache_l3_ckv, cache_l3_kpe, l0_norm, l0_w_in, l0_q_norm, l0_w_qb, l0_kv_norm, l0_w_kvb, l0_w_out, l1_norm, l1_w_in, l1_dw_w, l1_dw_b, l1_ln_g, l1_ln_b, l1_w_out, l2_norm, l2_w_in, l2_conv_w, l2_conv_b, l2_dt_bias, l2_a_log, l2_d_skip, l2_gnorm, l2_w_out, l3_norm, l3_w_in, l3_q_norm, l3_w_qb, l3_kv_norm, l3_w_kvb, l3_w_out, final_norm):
    batch, seq, d = x_prompt.shape
    nb, new, _ = x_sample.shape
    past = cache_l0_ckv.shape[1]
    xp = x_prompt.reshape(batch * seq, d)
    xs = x_sample.reshape(nb * new, d)
    ms = nb * new

    tabs_p = _rope_tables(jnp.arange(seq))
    cs, sn = _rope_tables(past + jnp.arange(new))
    tabs_s = (jnp.tile(cs, (nb, 1)), jnp.tile(sn, (nb, 1)))

    w0 = _mla_weights(l0_w_in, l0_w_qb, l0_w_kvb, l0_w_out)
    xp, xs, l0_ckv_p, l0_kpe_p, l0_ckv_s, l0_kpe_s = _mla_layer(
        xp, xs, l0_norm, l0_q_norm, l0_kv_norm, w0, cache_l0_ckv, cache_l0_kpe, tabs_p, tabs_s,
        batch=batch, seq=seq, fin=None, tag="l0")

    w1 = l1_w_in.astype(BF16)
    w1o = l1_w_out.astype(BF16)
    pad_hist = lambda h: jnp.pad(h, ((0, 0), (CONV_HIST - (CONV_K - 1), 0), (0, 0)))
    proj_p = norm_matmul(xp, l1_norm, w1, tm=512, tn=w1.shape[1] // 2, out_dtype=BF16, name="l1_inproj_p")
    ln1 = (l1_ln_g, l1_ln_b)
    gate_block = 2
    c_p, tail_p = conv_mixer(proj_p, jnp.zeros((batch, CONV_HIST, d), F32), l1_dw_w, l1_dw_b,
                             batch=batch, seq=seq, tt=256, name="l1_conv_p")
    xp = out_proj(c_p, proj_p, w1o, xp, None, tm=256, gate_block=gate_block, ln=ln1, name="l1_outproj_p")
    proj_s = norm_matmul(xs, l1_norm, w1, tm=ms, tn=w1.shape[1] // 2, out_dtype=BF16, name="l1_inproj_s")
    c_s, tail_s = conv_mixer(proj_s, pad_hist(state_l1_conv), l1_dw_w, l1_dw_b,
                             batch=nb, seq=new, tt=new, name="l1_conv_s")
    xs = out_proj(c_s, proj_s, w1o, xs, None, tm=ms, gate_block=gate_block, ln=ln1, name="l1_outproj_s")
    l1_conv_p = tail_p[:, CONV_HIST - (CONV_K - 1):]
    l1_conv_s = tail_s[:, CONV_HIST - (CONV_K - 1):]

    xp, xs, l2_conv_p, l2_ssm_p, l2_conv_s, l2_ssm_s = _ssd_layer(
        xp, xs, l2_norm, l2_w_in, l2_conv_w, l2_conv_b, l2_dt_bias, l2_a_log, l2_d_skip, l2_gnorm, l2_w_out,
        state_l2_conv, state_l2_ssm, batch=batch, seq=seq, nb=nb, new=new)

    w3 = _mla_weights(l3_w_in, l3_w_qb, l3_w_kvb, l3_w_out)
    yp, ys, l3_ckv_p, l3_kpe_p, l3_ckv_s, l3_kpe_s = _mla_layer(
        xp, xs, l3_norm, l3_q_norm, l3_kv_norm, w3, cache_l3_ckv, cache_l3_kpe, tabs_p, tabs_s,
        batch=batch, seq=seq, fin=final_norm, tag="l3")

    r3 = lambda a, n: a.reshape(n, -1, a.shape[-1])
    return (yp.reshape(batch, seq, d), ys.reshape(nb, new, d),
            r3(l0_ckv_p, batch), r3(l0_kpe_p, batch), r3(l0_ckv_s, nb), r3(l0_kpe_s, nb),
            l1_conv_p, l1_conv_s,
            l2_conv_p, l2_ssm_p, l2_conv_s, l2_ssm_s,
            r3(l3_ckv_p, batch), r3(l3_kpe_p, batch), r3(l3_ckv_s, nb), r3(l3_kpe_s, nb))
```

```python
import functools

import jax
import jax.numpy as jnp
import numpy as np
from jax import lax
from jax.experimental import pallas as pl
from jax.experimental.pallas import tpu as pltpu

F32 = jnp.float32
BF16 = jnp.bfloat16

EPS = 1e-6
ROPE_THETA = 10000.0
CHUNK = 64
HEADS = 16
NOPE = 128
ROPE = 64
HEAD_PAD = 256
MLA_SCALE = (NOPE + ROPE) ** -0.5
LOG2E = 1.4426950408889634
LORA = 512
CONV_K = 31
CONV_HIST = 32
SSM_GROUPS = 8
SSM_GROUP_HEADS = 8
SSM_HEAD_DIM = 64
SSM_GROUP_W = SSM_GROUP_HEADS * SSM_HEAD_DIM
D_STATE = 128
SSM_CONV_K = 4
SSM_HIST = 8
LANE = 128

VMEM_LIMIT_BYTES = 56 * 1024 * 1024


def _cparams(*sem):
    return pltpu.CompilerParams(dimension_semantics=sem, vmem_limit_bytes=VMEM_LIMIT_BYTES)


def _resident(shape, index_map):
    return pl.BlockSpec(shape, index_map, pipeline_mode=pl.Buffered(1))


def _sigmoid(x):
    return 0.5 * jnp.tanh(0.5 * x) + 0.5


def _silu(x):
    h = 0.5 * x
    return h + h * jnp.tanh(h)


def _split3(x):
    hi = x.astype(BF16)
    r = x - hi.astype(F32)
    mid = r.astype(BF16)
    return hi, mid, (r - mid.astype(F32)).astype(BF16)


def _dot_onehot_lhs(sel, x):
    hi, mid, lo = _split3(x)
    return _dot(sel, hi) + (_dot(sel, mid) + _dot(sel, lo))


def _dot_onehot_rhs(x, sel):
    hi, mid, lo = _split3(x)
    return _dot(hi, sel) + (_dot(mid, sel) + _dot(lo, sel))


def _rms(x, g):
    return (x * lax.rsqrt(jnp.mean(x * x, axis=-1, keepdims=True) + EPS)) * g


def _dot(a, b):
    return jnp.dot(a, b, preferred_element_type=F32)


def _dot_nt(a, b):
    return lax.dot_general(a, b, (((1,), (1,)), ((), ())), preferred_element_type=F32)


def _norm_matmul_kernel(x_ref, g_ref, w_ref, o_ref, *, col_chunk):
    xn = _rms(x_ref[...], g_ref[...]).astype(BF16)
    for c in range(0, o_ref.shape[1], col_chunk):
        o_ref[:, c:c + col_chunk] = _dot(xn, w_ref[:, c:c + col_chunk]).astype(o_ref.dtype)


def norm_matmul(x, g, w, *, tm, tn, out_dtype, name, n_out=None):
    m, k = x.shape
    n = w.shape[1] if n_out is None else n_out
    col_chunk = 256 if tn % 256 == 0 else LANE
    return pl.pallas_call(
        functools.partial(_norm_matmul_kernel, col_chunk=col_chunk),
        grid=(n // tn, m // tm),
        in_specs=[pl.BlockSpec((tm, k), lambda j, i: (i, 0)),
                  _resident((1, k), lambda j, i: (0, 0)),
                  _resident((k, tn), lambda j, i: (0, j))],
        out_specs=pl.BlockSpec((tm, tn), lambda j, i: (i, j)),
        out_shape=jax.ShapeDtypeStruct((m, n), out_dtype),
        compiler_params=_cparams("arbitrary", "arbitrary"),
        name=name,
    )(x, g.reshape(1, k), w)


def _out_proj_kernel(*refs, gated, layer_norm, final):
    refs = list(refs)
    a_ref = refs.pop(0)
    gate_ref = refs.pop(0) if gated else None
    lng_ref, lnb_ref = (refs.pop(0), refs.pop(0)) if layer_norm else (None, None)
    w_ref, res_ref = refs.pop(0), refs.pop(0)
    fin_ref = refs.pop(0) if final else None
    o_ref = refs.pop(0)
    a = a_ref[...]
    if layer_norm:
        cen = a - jnp.mean(a, axis=-1, keepdims=True)
        var = jnp.mean(cen * cen, axis=-1, keepdims=True)
        a = _silu((cen * lax.rsqrt(var + EPS)) * lng_ref[...] + lnb_ref[...])
    if gated:
        a = a.astype(F32) * _silu(gate_ref[...].astype(F32))
    y = res_ref[...] + _dot(a.astype(BF16), w_ref[...])
    if final:
        y = _rms(y, fin_ref[...])
    o_ref[...] = y


def out_proj(a, gate_src, w, res, fin, *, tm, name, gate_block=0, ln=None):
    m, k = a.shape
    n = w.shape[1]
    gated, final, layer_norm = gate_src is not None, fin is not None, ln is not None
    in_specs = [pl.BlockSpec((tm, k), lambda i: (i, 0))]
    args = [a]
    if gated:
        in_specs.append(pl.BlockSpec((tm, k), lambda i: (i, gate_block)))
        args.append(gate_src)
    if layer_norm:
        in_specs += [_resident((1, k), lambda i: (0, 0)), _resident((1, k), lambda i: (0, 0))]
        args += [ln[0].reshape(1, k), ln[1].reshape(1, k)]
    in_specs += [_resident((k, n), lambda i: (0, 0)), pl.BlockSpec((tm, n), lambda i: (i, 0))]
    args += [w, res]
    if final:
        in_specs.append(_resident((1, n), lambda i: (0, 0)))
        args.append(fin.reshape(1, n))
    return pl.pallas_call(
        functools.partial(_out_proj_kernel, gated=gated, layer_norm=layer_norm, final=final),
        grid=(m // tm,),
        in_specs=in_specs,
        out_specs=pl.BlockSpec((tm, n), lambda i: (i, 0)),
        out_shape=jax.ShapeDtypeStruct((m, n), F32),
        compiler_params=_cparams("parallel"),
        name=name,
    )(*args)


MLA_GATE_W = HEADS * NOPE
MLA_PROJ_W = MLA_GATE_W + 2 * LORA + 2 * LANE


def _mla_mid_kernel(*refs, with_kv):
    qa_ref, ckv_ref, kpe_ref, cos_ref, sin_ref, qn_ref, kvn_ref, wq_ref = refs[:8]
    refs = refs[8:]
    if with_kv:
        wk_ref, wvt_ref, q_out, ckv_out, kpe_out, k_out, v_out = refs
    else:
        q_out, ckv_out, kpe_out = refs
    cosp, sinp = cos_ref[...], sin_ref[...]
    q_scale = MLA_SCALE * LOG2E if with_kv else MLA_SCALE

    qa = _rms(qa_ref[...].astype(F32), qn_ref[...]).astype(BF16)
    qall = _dot(qa, wq_ref[...])
    for h in range(HEADS):
        lo, out = h * Q_COLS, h * HEAD_PAD
        q_out[:, out:out + LANE] = (qall[:, lo:lo + LANE] * q_scale).astype(BF16)
        rot = (qall[:, lo + LANE:lo + 2 * LANE] * cosp + qall[:, lo + 2 * LANE:lo + 3 * LANE] * sinp) * q_scale
        q_out[:, out + LANE:out + 2 * LANE] = rot.astype(BF16)

    ckv = _rms(ckv_ref[...].astype(F32), kvn_ref[...])
    ckv_out[...] = ckv
    kp = kpe_ref[...].astype(F32)
    kpe = kp[:, :LANE] * cosp + kp[:, LANE:] * sinp
    kpe_out[...] = kpe[:, :ROPE]
    if with_kv:
        ckv_b = ckv.astype(BF16)
        kpe_b = kpe.astype(BF16)
        kn = _dot(ckv_b, wk_ref[...])
        for h in range(HEADS):
            lo = h * LANE
            k_out[:, 2 * lo:2 * lo + LANE] = kn[:, lo:lo + LANE].astype(BF16)
            k_out[:, 2 * lo + LANE:2 * lo + 2 * LANE] = kpe_b
        v_out[0] = _dot_nt(wvt_ref[...], ckv_b).astype(BF16)


Q_COLS = 3 * LANE


def mla_mid(proj, cos_t, sin_t, q_norm, kv_norm, wq, wk, wvt, *, tm, with_kv, name):
    m = proj.shape[0]
    nt = cos_t.shape[0] // tm
    hw = HEADS * LANE
    in_specs = [
        pl.BlockSpec((tm, LORA), lambda i: (i, MLA_GATE_W // LORA)),
        pl.BlockSpec((tm, LORA), lambda i: (i, MLA_GATE_W // LORA + 1)),
        pl.BlockSpec((tm, 2 * LANE), lambda i: (i, (MLA_GATE_W + 2 * LORA) // (2 * LANE))),
        pl.BlockSpec((tm, LANE), lambda i: (i % nt, 0)),
        pl.BlockSpec((tm, LANE), lambda i: (i % nt, 0)),
        _resident((1, LORA), lambda i: (0, 0)),
        _resident((1, LORA), lambda i: (0, 0)),
        _resident((LORA, HEADS * Q_COLS), lambda i: (0, 0)),
    ]
    args = [proj, proj, proj, cos_t, sin_t, q_norm.reshape(1, LORA), kv_norm.reshape(1, LORA), wq]
    out_specs = [pl.BlockSpec((tm, HEADS * HEAD_PAD), lambda i: (i, 0)),
                 pl.BlockSpec((tm, LORA), lambda i: (i, 0)),
                 pl.BlockSpec((tm, ROPE), lambda i: (i, 0))]
    out_shape = [jax.ShapeDtypeStruct((m, HEADS * HEAD_PAD), BF16),
                 jax.ShapeDtypeStruct((m, LORA), F32),
                 jax.ShapeDtypeStruct((m, ROPE), F32)]
    if with_kv:
        assert tm == VT_COLS
        in_specs += [_resident((LORA, hw), lambda i: (0, 0)), _resident((hw, LORA), lambda i: (0, 0))]
        args += [wk, wvt]
        out_specs += [pl.BlockSpec((tm, HEADS * HEAD_PAD), lambda i: (i, 0)),
                      pl.BlockSpec((1, hw, tm), lambda i: (i, 0, 0))]
        out_shape += [jax.ShapeDtypeStruct((m, HEADS * HEAD_PAD), BF16),
                      jax.ShapeDtypeStruct((m // tm, hw, tm), BF16)]
    return pl.pallas_call(
        functools.partial(_mla_mid_kernel, with_kv=with_kv),
        grid=(m // tm,),
        in_specs=in_specs, out_specs=out_specs, out_shape=out_shape,
        compiler_params=_cparams("parallel"),
        name=name,
    )(*args)


NEG_BIG = -1e30


FLASH_HEADS = 2


VT_COLS = 256


def _flash_kernel(q_ref, k_ref, vt_ref, o_ref, *, tile):
    qi = pl.program_id(2)
    slabs = tile // VT_COLS

    def step(ki, carry, masked):
        start = pl.multiple_of(ki * tile, tile)
        out = []
        for h in range(FLASH_HEADS):
            m_prev, l_prev, acc = carry[h]
            q = q_ref[:, h * HEAD_PAD:(h + 1) * HEAD_PAD]
            st = _dot_nt(k_ref[pl.ds(start, tile), h * HEAD_PAD:(h + 1) * HEAD_PAD], q)
            if masked:
                kc = lax.broadcasted_iota(jnp.int32, st.shape, 0) // CHUNK
                qc = lax.broadcasted_iota(jnp.int32, st.shape, 1) // CHUNK
                st = jnp.where(kc <= qc, st, NEG_BIG)
            m_new = jnp.maximum(m_prev, jnp.max(st, axis=0, keepdims=True))
            alpha = jnp.exp2(m_prev - m_new)
            p = jnp.exp2(st - m_new)
            l_new = alpha * l_prev + jnp.sum(p, axis=0, keepdims=True)
            p = p.astype(BF16)
            pv = None
            for j in range(slabs):
                vt = vt_ref[ki * slabs + j, h * NOPE:(h + 1) * NOPE, :]
                part = _dot(vt, p[j * VT_COLS:(j + 1) * VT_COLS, :])
                pv = part if pv is None else pv + part
            out.append((m_new, l_new, alpha * acc + pv))
        return tuple(out)

    init = tuple((jnp.full((1, tile), NEG_BIG, F32), jnp.zeros((1, tile), F32), jnp.zeros((NOPE, tile), F32))
                 for _ in range(FLASH_HEADS))
    carry = lax.fori_loop(0, qi, lambda ki, c: step(ki, c, False), init)
    carry = step(qi, carry, True)
    for h in range(FLASH_HEADS):
        _, l_fin, acc = carry[h]
        o_ref[:, h * NOPE:(h + 1) * NOPE] = (acc / l_fin).T.astype(o_ref.dtype)


def flash_attention(q, k, vt, *, batch, seq, tile, name):
    nq = seq // tile
    hq, hv = FLASH_HEADS * HEAD_PAD, FLASH_HEADS * NOPE
    return pl.pallas_call(
        functools.partial(_flash_kernel, tile=tile),
        grid=(batch, HEADS // FLASH_HEADS, nq),
        in_specs=[pl.BlockSpec((tile, hq), lambda b, h, i: (b * nq + i, h)),
                  pl.BlockSpec((seq, hq), lambda b, h, i: (b, h)),
                  pl.BlockSpec((seq // VT_COLS, hv, VT_COLS), lambda b, h, i: (b, h, 0))],
        out_specs=pl.BlockSpec((tile, hv), lambda b, h, i: (b * nq + i, h)),
        out_shape=jax.ShapeDtypeStruct((batch * seq, HEADS * NOPE), BF16),
        compiler_params=_cparams("parallel", "parallel", "arbitrary"),
        name=name,
    )(q, k, vt)


def _sample_attn_kernel(q_ref, cnew_ref, pnew_ref, ccache_ref, pcache_ref, wk_ref, wvt_ref, o_ref,
                        kc_scr, kp_scr, *, past, new, pad):
    total = past + pad
    half = past // 2
    kc_scr[0:half, :] = ccache_ref[0, :, :LORA].astype(BF16)
    kc_scr[half:past, :] = ccache_ref[0, :, LORA:].astype(BF16)
    pc = pcache_ref[0].astype(BF16)
    kp_scr[0:half, :] = pc[:, :ROPE]
    kp_scr[half:past, :] = pc[:, ROPE:]
    kc_scr[past:total, :] = jnp.zeros((pad, LORA), BF16)
    kp_scr[past:total, :] = jnp.zeros((pad, ROPE), BF16)
    kc_scr[past:past + new, :] = cnew_ref[...].astype(BF16)
    kp_scr[past:past + new, :] = pnew_ref[...].astype(BF16)

    q = q_ref[...]
    qlat, qpe = [], []
    for h in range(HEADS):
        lo = h * HEAD_PAD
        qlat.append(_dot_nt(q[:, lo:lo + NOPE], wk_ref[:, h * NOPE:(h + 1) * NOPE]))
        qpe.append(q[:, lo + NOPE:lo + NOPE + ROPE])
    qlat = jnp.concatenate(qlat, axis=0).astype(BF16)
    qpe = jnp.concatenate(qpe, axis=0)
    kc = kc_scr[...]
    s = _dot_nt(qlat, kc) + _dot_nt(qpe, kp_scr[...])
    col = lax.broadcasted_iota(jnp.int32, s.shape, 1)
    s = jnp.where(col < past + new, s, NEG_BIG)
    m = jnp.max(s, axis=-1, keepdims=True)
    p = jnp.exp(s - m)
    l = jnp.sum(p, axis=-1, keepdims=True)
    olat = (_dot(p.astype(BF16), kc) / l).astype(BF16)
    for h in range(HEADS):
        o_ref[:, h * NOPE:(h + 1) * NOPE] = _dot_nt(
            olat[h * new:(h + 1) * new, :], wvt_ref[h * NOPE:(h + 1) * NOPE, :]).astype(o_ref.dtype)


def sample_attention(q, ckv_new, kpe_new, ckv_cache, kpe_cache, wk, wvt, *, name):
    nb, past, _ = ckv_cache.shape
    new = q.shape[0] // nb
    pad = LANE
    hw = HEADS * NOPE
    return pl.pallas_call(
        functools.partial(_sample_attn_kernel, past=past, new=new, pad=pad),
        grid=(nb,),
        in_specs=[pl.BlockSpec((new, HEADS * HEAD_PAD), lambda b: (b, 0)),
                  pl.BlockSpec((new, LORA), lambda b: (b, 0)),
                  pl.BlockSpec((new, ROPE), lambda b: (b, 0)),
                  pl.BlockSpec((1, past // 2, 2 * LORA), lambda b: (b, 0, 0)),
                  pl.BlockSpec((1, past // 2, 2 * ROPE), lambda b: (b, 0, 0)),
                  _resident((LORA, hw), lambda b: (0, 0)),
                  _resident((hw, LORA), lambda b: (0, 0))],
        out_specs=pl.BlockSpec((new, hw), lambda b: (b, 0)),
        out_shape=jax.ShapeDtypeStruct((nb * new, hw), BF16),
        scratch_shapes=[pltpu.VMEM((past + pad, LORA), BF16), pltpu.VMEM((past + pad, ROPE), BF16)],
        compiler_params=_cparams("arbitrary"),
        name=name,
    )(q, ckv_new, kpe_new, ckv_cache.reshape(nb, past // 2, 2 * LORA),
      kpe_cache.reshape(nb, past // 2, 2 * ROPE), wk, wvt)


SUBLANES = 8
CONV_COLS = 512


def _conv_kernel(val_ref, glu_ref, hist_ref, dw_ref, dwb_ref, y_ref, tail_ref, ubuf, shifted, *, tt):
    i = pl.program_id(2)

    @pl.when(i == 0)
    def _():
        ubuf[0:CONV_HIST, :] = hist_ref[0]

    ubuf[CONV_HIST:CONV_HIST + tt, :] = val_ref[...].astype(F32) * _sigmoid(glu_ref[...].astype(F32))
    span = tt + CONV_HIST - SUBLANES
    for r in range(1, SUBLANES):
        shifted[r - 1] = ubuf[r:r + span, :]
    first = CONV_HIST - (CONV_K - 1)
    for c in range(0, ubuf.shape[1], LANE):
        acc = None
        for k in range(CONV_K):
            q, r = divmod(first + k, SUBLANES)
            rows = pl.ds(q * SUBLANES, tt)
            src = ubuf[rows, c:c + LANE] if r == 0 else shifted[r - 1, rows, c:c + LANE]
            term = src * dw_ref[k:k + 1, c:c + LANE]
            acc = term if acc is None else acc + term
        y_ref[:, c:c + LANE] = acc + dwb_ref[:, c:c + LANE]
    tail = ubuf[tt:tt + CONV_HIST, :]
    tail_ref[0] = tail
    ubuf[0:CONV_HIST, :] = tail


def conv_mixer(proj, hist, dw_w, dw_b, *, batch, seq, tt, name):
    d = dw_w.shape[1]
    nt, cw = seq // tt, CONV_COLS
    nc = d // cw
    return pl.pallas_call(
        functools.partial(_conv_kernel, tt=tt),
        grid=(batch, nc, nt),
        in_specs=[pl.BlockSpec((tt, cw), lambda b, j, i: (b * nt + i, j)),
                  pl.BlockSpec((tt, cw), lambda b, j, i: (b * nt + i, nc + j)),
                  pl.BlockSpec((1, CONV_HIST, cw), lambda b, j, i: (b, 0, j)),
                  pl.BlockSpec((CONV_K, cw), lambda b, j, i: (0, j)),
                  pl.BlockSpec((1, cw), lambda b, j, i: (0, j))],
        out_specs=[pl.BlockSpec((tt, cw), lambda b, j, i: (b * nt + i, j)),
                   pl.BlockSpec((1, CONV_HIST, cw), lambda b, j, i: (b, 0, j))],
        out_shape=[jax.ShapeDtypeStruct((batch * seq, d), F32),
                   jax.ShapeDtypeStruct((batch, CONV_HIST, d), F32)],
        scratch_shapes=[pltpu.VMEM((CONV_HIST + tt, cw), F32),
                        pltpu.VMEM((SUBLANES - 1, tt + CONV_HIST - SUBLANES, cw), F32)],
        compiler_params=_cparams("parallel", "parallel", "arbitrary"),
        name=name,
    )(proj, proj, hist, dw_w, dw_b.reshape(1, d))


def _softplus(x):
    return jnp.maximum(x, 0.0) + jnp.log1p(jnp.exp(-jnp.abs(x)))


def _ssd_kernel(z_ref, x_ref, b_ref, c_ref, dt_ref, hx_ref, hb_ref, hc_ref, st0_ref,
                cwx_ref, cwb_ref, cwc_ref, cbx_ref, cbb_ref, cbc_ref,
                dtal_ref, dskip_ref, gn_ref, expb_ref,
                y_ref, tx_ref, tb_ref, tc_ref, st_ref,
                xbuf, bbuf, cbuf, state, *, L, valid):
    ci = pl.program_id(2)

    @pl.when(ci == 0)
    def _():
        xbuf[0:SSM_HIST, :] = hx_ref[0]
        bbuf[0:SSM_HIST, :] = hb_ref[0]
        cbuf[0:SSM_HIST, :] = hc_ref[0]
        state[...] = st0_ref[0].T

    def short_conv(buf, new_ref, w_ref, bias_ref, tail_ref):
        buf[SSM_HIST:SSM_HIST + L, :] = new_ref[...].astype(F32)
        off = SSM_HIST - (SSM_CONV_K - 1)
        acc = buf[off:off + L, :] * w_ref[0:1, :]
        for k in range(1, SSM_CONV_K):
            acc = acc + buf[off + k:off + k + L, :] * w_ref[k:k + 1, :]
        tail = buf[valid:valid + SSM_HIST, :]
        tail_ref[0] = tail
        buf[0:SSM_HIST, :] = tail
        return _silu(acc + bias_ref[...])

    x = short_conv(xbuf, x_ref, cwx_ref, cbx_ref, tx_ref)
    bm = short_conv(bbuf, b_ref, cwb_ref, cbb_ref, tb_ref)
    cm = short_conv(cbuf, c_ref, cwc_ref, cbc_ref, tc_ref)

    row = lax.broadcasted_iota(jnp.int32, (L, L), 0)
    col = lax.broadcasted_iota(jnp.int32, (L, L), 1)
    causal = col <= row
    dt = _softplus(dt_ref[...] + dtal_ref[0, 0:1, :])
    if valid < L:
        dt = jnp.where(lax.broadcasted_iota(jnp.int32, dt.shape, 0) < valid, dt, 0.0)
    da = dt * (-jnp.exp(dtal_ref[0, 1:2, :]))
    acs = _dot_onehot_lhs(causal.astype(BF16), da)
    acs_last = acs[L - 1:L, :]
    eacs = jnp.exp(acs)
    dte = jnp.exp(acs_last - acs) * dt
    acs_t = acs.T
    dt_t = dt.T

    eacs_x = _dot(eacs.astype(BF16), expb_ref[...])
    dte_x = _dot(dte.astype(BF16), expb_ref[...])
    cdec_x = _dot_onehot_rhs(jnp.broadcast_to(jnp.exp(acs_last), (SUBLANES, LANE)), expb_ref[...])[0:1, :]

    x_b = x.astype(BF16)
    bm_b = bm.astype(BF16)
    cm_b = cm.astype(BF16)
    cb = _dot_nt(cm_b, bm_b)
    st = state[...]
    y_off = _dot(cm_b, st.astype(BF16)) * eacs_x

    low_half = lax.broadcasted_iota(jnp.int32, (L, LANE), 1) < SSM_HEAD_DIM
    pairs = []
    for pr in range(SSM_GROUP_HEADS // 2):
        xp = x_b[:, pr * LANE:(pr + 1) * LANE]
        ys = []
        for e in (2 * pr, 2 * pr + 1):
            seg = jnp.where(causal, jnp.exp(acs[:, e:e + 1] - acs_t[e:e + 1, :]), 0.0)
            mm = (cb * seg * dt_t[e:e + 1, :]).astype(BF16)
            ys.append(_dot(mm, xp))
        pairs.append(jnp.where(low_half, ys[0], ys[1]))
    y = jnp.concatenate(pairs, axis=1) + y_off + dskip_ref[...] * x

    state[...] = st * cdec_x + _dot(bm.T.astype(BF16), (x * dte_x).astype(BF16))

    yz = y * _silu(z_ref[...].astype(F32))
    yn = (yz * lax.rsqrt(jnp.mean(yz * yz, axis=-1, keepdims=True) + EPS)) * gn_ref[...]
    y_ref[...] = yn.astype(y_ref.dtype)
    st_ref[0] = state[...].T


def ssd_mixer(proj, dtp, hist, st0, conv_w, conv_b, dtal, dskip, gnorm, *, batch, nchunks, L, valid, name):
    gw, ns = SSM_GROUP_W, D_STATE
    xb = (SSM_GROUPS * gw) // gw
    bb = (2 * SSM_GROUPS * gw) // ns
    cb = bb + SSM_GROUPS
    pb = (SSM_GROUPS * gw) // ns
    pc = pb + SSM_GROUPS
    rows = batch * nchunks * L
    rmap = lambda off: (lambda b, g, c: (b * nchunks + c, off + g))
    hmap = lambda off: (lambda b, g, c: (b, 0, off + g))
    pmap = lambda off: (lambda b, g, c: (0, off + g))
    in_specs = [
        pl.BlockSpec((L, gw), rmap(0)), pl.BlockSpec((L, gw), rmap(xb)),
        pl.BlockSpec((L, ns), rmap(bb)), pl.BlockSpec((L, ns), rmap(cb)),
        pl.BlockSpec((L, LANE), rmap(0)),
        pl.BlockSpec((1, SSM_HIST, gw), hmap(0)), pl.BlockSpec((1, SSM_HIST, ns), hmap(pb)),
        pl.BlockSpec((1, SSM_HIST, ns), hmap(pc)),
        pl.BlockSpec((1, gw, ns), lambda b, g, c: (b, g, 0)),
        pl.BlockSpec((SSM_CONV_K, gw), pmap(0)), pl.BlockSpec((SSM_CONV_K, ns), pmap(pb)),
        pl.BlockSpec((SSM_CONV_K, ns), pmap(pc)),
        pl.BlockSpec((1, gw), pmap(0)), pl.BlockSpec((1, ns), pmap(pb)), pl.BlockSpec((1, ns), pmap(pc)),
        pl.BlockSpec((1, 2, LANE), lambda b, g, c: (g, 0, 0)),
        pl.BlockSpec((1, gw), pmap(0)), pl.BlockSpec((1, gw), pmap(0)),
        _resident((LANE, gw), lambda b, g, c: (0, 0)),
    ]
    expand = (jnp.arange(gw)[None, :] // SSM_HEAD_DIM) == jnp.arange(LANE)[:, None]
    out_specs = [
        pl.BlockSpec((L, gw), rmap(0)),
        pl.BlockSpec((1, SSM_HIST, gw), hmap(0)), pl.BlockSpec((1, SSM_HIST, ns), hmap(0)),
        pl.BlockSpec((1, SSM_HIST, ns), hmap(0)),
        pl.BlockSpec((1, gw, ns), lambda b, g, c: (b, g, 0)),
    ]
    out_shape = [
        jax.ShapeDtypeStruct((rows, SSM_GROUPS * gw), BF16),
        jax.ShapeDtypeStruct((batch, SSM_HIST, SSM_GROUPS * gw), F32),
        jax.ShapeDtypeStruct((batch, SSM_HIST, SSM_GROUPS * ns), F32),
        jax.ShapeDtypeStruct((batch, SSM_HIST, SSM_GROUPS * ns), F32),
        jax.ShapeDtypeStruct((batch, SSM_GROUPS * gw, ns), F32),
    ]
    return pl.pallas_call(
        functools.partial(_ssd_kernel, L=L, valid=valid),
        grid=(batch, SSM_GROUPS, nchunks),
        in_specs=in_specs, out_specs=out_specs, out_shape=out_shape,
        scratch_shapes=[pltpu.VMEM((SSM_HIST + L, gw), F32), pltpu.VMEM((SSM_HIST + L, ns), F32),
                        pltpu.VMEM((SSM_HIST + L, ns), F32), pltpu.VMEM((ns, gw), F32)],
        compiler_params=_cparams("parallel", "parallel", "arbitrary"),
        name=name,
    )(proj, proj, proj, proj, dtp, hist, hist, hist, st0,
      conv_w, conv_w, conv_w, conv_b, conv_b, conv_b, dtal, dskip, gnorm, expand.astype(BF16))


def _rope_tables(pos):
    half = ROPE // 2
    freqs = np.float32(ROPE_THETA) ** (-np.arange(half, dtype=np.float32) / np.float32(half))
    ang = pos.astype(np.float32)[:, None] * freqs[None, :]
    zeros = np.zeros((pos.shape[0], LANE - ROPE), np.float32)
    cos, sin = np.cos(ang), np.sin(ang)
    return np.concatenate([cos, cos, zeros], axis=1), np.concatenate([sin, sin, zeros], axis=1)


def _rot_cols(w):
    half = ROPE // 2
    return jnp.concatenate([-w[..., half:], w[..., :half]], axis=-1)


def _mla_weights(w_in, w_qb, w_kvb, w_out):
    d = w_in.shape[0]
    qa, ckv = w_in[:, :LORA], w_in[:, LORA:2 * LORA]
    kpe, gate = w_in[:, 2 * LORA:2 * LORA + ROPE], w_in[:, 2 * LORA + ROPE:]
    z = jnp.zeros((d, LANE - ROPE), F32)
    w0 = jnp.concatenate([gate, qa, ckv, kpe, z, _rot_cols(kpe), z], axis=1).astype(BF16)
    hw = HEADS * NOPE
    r = w_qb[:, :, NOPE:]
    zr = jnp.zeros((LORA, HEADS, LANE - ROPE), F32)
    wq = jnp.concatenate([w_qb[:, :, :NOPE], r, zr, _rot_cols(r), zr], axis=-1)
    return dict(
        w0=w0,
        wq=wq.reshape(LORA, HEADS * Q_COLS).astype(BF16),
        wk=w_kvb[:, :, :NOPE].reshape(LORA, hw).astype(BF16),
        wvt=w_kvb[:, :, NOPE:].reshape(LORA, hw).T.astype(BF16),
        wo=w_out.astype(BF16),
    )


def _mla_layer(xp, xs, norm_g, q_norm, kv_norm, wts, ckv_cache, kpe_cache, tabs_p, tabs_s,
               *, batch, seq, fin, tag):
    mid = functools.partial(mla_mid, q_norm=q_norm, kv_norm=kv_norm, wq=wts["wq"], wk=wts["wk"], wvt=wts["wvt"])
    proj_p = norm_matmul(xp, norm_g, wts["w0"], tm=512, tn=MLA_PROJ_W, out_dtype=BF16, name=tag + "_inproj_p")
    q, ckv_p, kpe_p, k, vt = mid(proj_p, *tabs_p, tm=VT_COLS, with_kv=True, name=tag + "_mid_p")
    o = flash_attention(q, k, vt, batch=batch, seq=seq, tile=min(seq, 1024), name=tag + "_flash")
    xp = out_proj(o, proj_p, wts["wo"], xp, fin, tm=256, name=tag + "_outproj_p")

    ms = xs.shape[0]
    proj_s = norm_matmul(xs, norm_g, wts["w0"], tm=ms, tn=MLA_PROJ_W, out_dtype=BF16, name=tag + "_inproj_s")
    qs, ckv_s, kpe_s = mid(proj_s, *tabs_s, tm=ms, with_kv=False, name=tag + "_mid_s")
    os_ = sample_attention(qs, ckv_s, kpe_s, ckv_cache, kpe_cache, wts["wk"], wts["wvt"], name=tag + "_attn_s")
    xs = out_proj(os_, proj_s, wts["wo"], xs, fin, tm=ms, name=tag + "_outproj_s")
    return xp, xs, ckv_p, kpe_p, ckv_s, kpe_s


def _ssd_layer(xp, xs, l2_norm, l2_w_in, l2_conv_w, l2_conv_b, l2_dt_bias, l2_a_log, l2_d_skip, l2_gnorm, l2_w_out,
               state_l2_conv, state_l2_ssm, *, batch, seq, nb, new, chunk_p=256, chunk_s=None):
    d = xp.shape[1]
    ms = nb * new
    inner = SSM_GROUPS * SSM_GROUP_W
    bc = SSM_GROUPS * D_STATE
    main_w = 2 * inner + 2 * bc
    w2 = l2_w_in.astype(BF16)
    w2dt = jnp.pad(l2_w_in[:, main_w:].reshape(d, SSM_GROUPS, SSM_GROUP_HEADS),
                   ((0, 0), (0, 0), (0, LANE - SSM_GROUP_HEADS))).reshape(d, SSM_GROUPS * LANE).astype(BF16)
    w2o = l2_w_out.astype(BF16)
    dtal = jnp.pad(jnp.stack([l2_dt_bias, l2_a_log]).reshape(2, SSM_GROUPS, SSM_GROUP_HEADS).transpose(1, 0, 2),
                   ((0, 0), (0, 0), (0, LANE - SSM_GROUP_HEADS)))
    dskip = jnp.repeat(l2_d_skip, SSM_HEAD_DIM).reshape(1, inner)
    hist_rows = SSM_CONV_K - 1

    def run_ssd(proj, dtp, hist, st0, nbatch, nchunks, L, valid, name):
        hist = jnp.pad(hist, ((0, 0), (SSM_HIST - hist_rows, 0), (0, 0)))
        y, tx, tb, tc, st = ssd_mixer(
            proj, dtp, hist, st0.reshape(nbatch, inner, D_STATE), l2_conv_w, l2_conv_b.reshape(1, -1), dtal, dskip,
            l2_gnorm.reshape(1, inner), batch=nbatch, nchunks=nchunks, L=L, valid=valid, name=name)
        tail = jnp.concatenate([tx, tb, tc], axis=-1)[:, SSM_HIST - hist_rows:]
        return y, tail, st.reshape(nbatch, SSM_GROUPS * SSM_GROUP_HEADS, SSM_HEAD_DIM, D_STATE)

    proj_p = norm_matmul(xp, l2_norm, w2, tm=256, tn=main_w // 2, n_out=main_w, out_dtype=BF16, name="l2_inproj_p")
    dt_p = norm_matmul(xp, l2_norm, w2dt, tm=512, tn=SSM_GROUPS * LANE, out_dtype=F32, name="l2_dtproj_p")
    y_p, l2_conv_p, l2_ssm_p = run_ssd(
        proj_p, dt_p, jnp.zeros((batch, hist_rows, inner + 2 * bc), F32),
        jnp.zeros((batch, inner, D_STATE), F32), batch, seq // chunk_p, chunk_p, chunk_p, "l2_ssd_p")
    xp = out_proj(y_p, None, w2o, xp, None, tm=256, name="l2_outproj_p")

    proj_s = norm_matmul(xs, l2_norm, w2, tm=ms, tn=main_w // 2, n_out=main_w, out_dtype=BF16, name="l2_inproj_s")
    dt_s = norm_matmul(xs, l2_norm, w2dt, tm=ms, tn=SSM_GROUPS * LANE, out_dtype=F32, name="l2_dtproj_s")
    L = chunk_s if chunk_s is not None else -(-new // SUBLANES) * SUBLANES
    pad_rows = lambda a: jnp.pad(a.reshape(nb, new, a.shape[1]), ((0, 0), (0, L - new), (0, 0))).reshape(nb * L, a.shape[1])
    y_s, l2_conv_s, l2_ssm_s = run_ssd(pad_rows(proj_s), pad_rows(dt_s), state_l2_conv, state_l2_ssm,
                                       nb, 1, L, new, "l2_ssd_s")
    y_s = y_s.reshape(nb, L, inner)[:, :new].reshape(ms, inner)
    xs = out_proj(y_s, None, w2o, xs, None, tm=ms, name="l2_outproj_s")

    return xp, xs, l2_conv_p, l2_ssm_p, l2_conv_s, l2_ssm_s


def kernel(x_prompt, x_sample, cache_l0_ckv, cache_l0_kpe, state_l1_conv, state_l2_conv, state_l2_ssm, cache_l3_ckv, cache_l3_kpe, l0_norm, l0_w_in, l0_q_norm, l0_w_qb, l0_kv_norm, l0_w_kvb, l0_w_out, l1_norm, l1_w_in, l1_dw_w, l1_dw_b, l1_ln_g, l1_ln_b, l1_w_out, l2_norm, l2_w_in, l2_conv_w, l2_conv_b, l2_dt_bias, l2_a_log, l2_d_skip, l2_gnorm, l2_w_out, l3_norm, l3_w_in, l3_q_norm, l3_w_qb, l3_kv_norm, l3_w_kvb, l3_w_out, final_norm):
    batch, seq, d = x_prompt.shape
    nb, new, _ = x_sample.shape
    past = cache_l0_ckv.shape[1]
    xp = x_prompt.reshape(batch * seq, d)
    xs = x_sample.reshape(nb * new, d)
    ms = nb * new

    tabs_p = _rope_tables(np.arange(seq))
    cs, sn = _rope_tables(past + np.arange(new))
    tabs_s = (np.tile(cs, (nb, 1)), np.tile(sn, (nb, 1)))

    w0 = _mla_weights(l0_w_in, l0_w_qb, l0_w_kvb, l0_w_out)
    xp, xs, l0_ckv_p, l0_kpe_p, l0_ckv_s, l0_kpe_s = _mla_layer(
        xp, xs, l0_norm, l0_q_norm, l0_kv_norm, w0, cache_l0_ckv, cache_l0_kpe, tabs_p, tabs_s,
        batch=batch, seq=seq, fin=None, tag="l0")

    w1 = l1_w_in.astype(BF16)
    w1o = l1_w_out.astype(BF16)
    pad_hist = lambda h: jnp.pad(h, ((0, 0), (CONV_HIST - (CONV_K - 1), 0), (0, 0)))
    proj_p = norm_matmul(xp, l1_norm, w1, tm=512, tn=w1.shape[1] // 2, out_dtype=BF16, name="l1_inproj_p")
    ln1 = (l1_ln_g, l1_ln_b)
    gate_block = 2
    c_p, tail_p = conv_mixer(proj_p, jnp.zeros((batch, CONV_HIST, d), F32), l1_dw_w, l1_dw_b,
                             batch=batch, seq=seq, tt=256, name="l1_conv_p")
    xp = out_proj(c_p, proj_p, w1o, xp, None, tm=256, gate_block=gate_block, ln=ln1, name="l1_outproj_p")
    proj_s = norm_matmul(xs, l1_norm, w1, tm=ms, tn=w1.shape[1] // 2, out_dtype=BF16, name="l1_inproj_s")
    c_s, tail_s = conv_mixer(proj_s, pad_hist(state_l1_conv), l1_dw_w, l1_dw_b,
                             batch=nb, seq=new, tt=new, name="l1_conv_s")
    xs = out_proj(c_s, proj_s, w1o, xs, None, tm=ms, gate_block=gate_block, ln=ln1, name="l1_outproj_s")
    l1_conv_p = tail_p[:, CONV_HIST - (CONV_K - 1):]
    l1_conv_s = tail_s[:, CONV_HIST - (CONV_K - 1):]

    xp, xs, l2_conv_p, l2_ssm_p, l2_conv_s, l2_ssm_s = _ssd_layer(
        xp, xs, l2_norm, l2_w_in, l2_conv_w, l2_conv_b, l2_dt_bias, l2_a_log, l2_d_skip, l2_gnorm, l2_w_out,
        state_l2_conv, state_l2_ssm, batch=batch, seq=seq, nb=nb, new=new)

    w3 = _mla_weights(l3_w_in, l3_w_qb, l3_w_kvb, l3_w_out)
    yp, ys, l3_ckv_p, l3_kpe_p, l3_ckv_s, l3_kpe_s = _mla_layer(
        xp, xs, l3_norm, l3_q_norm, l3_kv_norm, w3, cache_l3_ckv, cache_l3_kpe, tabs_p, tabs_s,
        batch=batch, seq=seq, fin=final_norm, tag="l3")

    r3 = lambda a, n: a.reshape(n, -1, a.shape[-1])
    return (yp.reshape(batch, seq, d), ys.reshape(nb, new, d),
            r3(l0_ckv_p, batch), r3(l0_kpe_p, batch), r3(l0_ckv_s, nb), r3(l0_kpe_s, nb),
            l1_conv_p, l1_conv_s,
            l2_conv_p, l2_ssm_p, l2_conv_s, l2_ssm_s,
            r3(l3_ckv_p, batch), r3(l3_kpe_p, batch), r3(l3_ckv_s, nb), r3(l3_kpe_s, nb))
```

```python
import functools

import jax
import jax.numpy as jnp
import numpy as np
from jax import lax
from jax.experimental import pallas as pl
from jax.experimental.pallas import tpu as pltpu

F32 = jnp.float32
BF16 = jnp.bfloat16

EPS = 1e-6
ROPE_THETA = 10000.0
CHUNK = 64
HEADS = 16
NOPE = 128
ROPE = 64
HEAD_PAD = 256
MLA_SCALE = (NOPE + ROPE) ** -0.5
LOG2E = 1.4426950408889634
LORA = 512
CONV_K = 31
CONV_HIST = 32
SSM_GROUPS = 8
SSM_GROUP_HEADS = 8
SSM_HEAD_DIM = 64
SSM_GROUP_W = SSM_GROUP_HEADS * SSM_HEAD_DIM
D_STATE = 128
SSM_CONV_K = 4
SSM_HIST = 8
LANE = 128

VMEM_LIMIT_BYTES = 56 * 1024 * 1024


def _cparams(*sem):
    return pltpu.CompilerParams(dimension_semantics=sem, vmem_limit_bytes=VMEM_LIMIT_BYTES)


def _resident(shape, index_map):
    return pl.BlockSpec(shape, index_map, pipeline_mode=pl.Buffered(1))


def _sigmoid(x):
    return 0.5 * jnp.tanh(0.5 * x) + 0.5


def _silu(x):
    h = 0.5 * x
    return h + h * jnp.tanh(h)


def _split3(x):
    hi = x.astype(BF16)
    r = x - hi.astype(F32)
    mid = r.astype(BF16)
    return hi, mid, (r - mid.astype(F32)).astype(BF16)


def _dot_onehot_lhs(sel, x):
    hi, mid, lo = _split3(x)
    return _dot(sel, hi) + (_dot(sel, mid) + _dot(sel, lo))


def _dot_onehot_rhs(x, sel):
    hi, mid, lo = _split3(x)
    return _dot(hi, sel) + (_dot(mid, sel) + _dot(lo, sel))


def _rms(x, g):
    return (x * lax.rsqrt(jnp.mean(x * x, axis=-1, keepdims=True) + EPS)) * g


def _dot(a, b):
    return jnp.dot(a, b, preferred_element_type=F32)


def _dot_nt(a, b):
    return lax.dot_general(a, b, (((1,), (1,)), ((), ())), preferred_element_type=F32)


def _norm_matmul_kernel(x_ref, g_ref, w_ref, o_ref, *, col_chunk):
    xn = _rms(x_ref[...], g_ref[...]).astype(BF16)
    for c in range(0, o_ref.shape[1], col_chunk):
        o_ref[:, c:c + col_chunk] = _dot(xn, w_ref[:, c:c + col_chunk]).astype(o_ref.dtype)


def norm_matmul(x, g, w, *, tm, tn, out_dtype, name, n_out=None):
    m, k = x.shape
    n = w.shape[1] if n_out is None else n_out
    col_chunk = 256 if tn % 256 == 0 else LANE
    return pl.pallas_call(
        functools.partial(_norm_matmul_kernel, col_chunk=col_chunk),
        grid=(n // tn, m // tm),
        in_specs=[pl.BlockSpec((tm, k), lambda j, i: (i, 0)),
                  _resident((1, k), lambda j, i: (0, 0)),
                  _resident((k, tn), lambda j, i: (0, j))],
        out_specs=pl.BlockSpec((tm, tn), lambda j, i: (i, j)),
        out_shape=jax.ShapeDtypeStruct((m, n), out_dtype),
        compiler_params=_cparams("arbitrary", "arbitrary"),
        name=name,
    )(x, g.reshape(1, k), w)


def _out_proj_kernel(*refs, gated, layer_norm, final):
    refs = list(refs)
    a_ref = refs.pop(0)
    gate_ref = refs.pop(0) if gated else None
    lng_ref, lnb_ref = (refs.pop(0), refs.pop(0)) if layer_norm else (None, None)
    w_ref, res_ref = refs.pop(0), refs.pop(0)
    fin_ref = refs.pop(0) if final else None
    o_ref = refs.pop(0)
    a = a_ref[...]
    if layer_norm:
        cen = a - jnp.mean(a, axis=-1, keepdims=True)
        var = jnp.mean(cen * cen, axis=-1, keepdims=True)
        a = _silu((cen * lax.rsqrt(var + EPS)) * lng_ref[...] + lnb_ref[...])
    if gated:
        a = a.astype(F32) * _silu(gate_ref[...].astype(F32))
    y = res_ref[...] + _dot(a.astype(BF16), w_ref[...])
    if final:
        y = _rms(y, fin_ref[...])
    o_ref[...] = y


def out_proj(a, gate_src, w, res, fin, *, tm, name, gate_block=0, ln=None):
    m, k = a.shape
    n = w.shape[1]
    gated, final, layer_norm = gate_src is not None, fin is not None, ln is not None
    in_specs = [pl.BlockSpec((tm, k), lambda i: (i, 0))]
    args = [a]
    if gated:
        in_specs.append(pl.BlockSpec((tm, k), lambda i: (i, gate_block)))
        args.append(gate_src)
    if layer_norm:
        in_specs += [_resident((1, k), lambda i: (0, 0)), _resident((1, k), lambda i: (0, 0))]
        args += [ln[0].reshape(1, k), ln[1].reshape(1, k)]
    in_specs += [_resident((k, n), lambda i: (0, 0)), pl.BlockSpec((tm, n), lambda i: (i, 0))]
    args += [w, res]
    if final:
        in_specs.append(_resident((1, n), lambda i: (0, 0)))
        args.append(fin.reshape(1, n))
    return pl.pallas_call(
        functools.partial(_out_proj_kernel, gated=gated, layer_norm=layer_norm, final=final),
        grid=(m // tm,),
        in_specs=in_specs,
        out_specs=pl.BlockSpec((tm, n), lambda i: (i, 0)),
        out_shape=jax.ShapeDtypeStruct((m, n), F32),
        compiler_params=_cparams("parallel"),
        name=name,
    )(*args)


MLA_GATE_W = HEADS * NOPE
MLA_PROJ_W = MLA_GATE_W + 2 * LORA + 2 * LANE


def _mla_mid_kernel(*refs, with_kv):
    qa_ref, ckv_ref, kpe_ref, cos_ref, sin_ref, qn_ref, kvn_ref, wq_ref = refs[:8]
    refs = refs[8:]
    if with_kv:
        wk_ref, wvt_ref, q_out, ckv_out, kpe_out, k_out, v_out = refs
    else:
        q_out, ckv_out, kpe_out = refs
    cosp, sinp = cos_ref[...], sin_ref[...]
    q_scale = MLA_SCALE * LOG2E if with_kv else MLA_SCALE

    qa = _rms(qa_ref[...].astype(F32), qn_ref[...]).astype(BF16)
    qall = _dot(qa, wq_ref[...])
    for h in range(HEADS):
        lo, out = h * Q_COLS, h * HEAD_PAD
        q_out[:, out:out + LANE] = (qall[:, lo:lo + LANE] * q_scale).astype(BF16)
        rot = (qall[:, lo + LANE:lo + 2 * LANE] * cosp + qall[:, lo + 2 * LANE:lo + 3 * LANE] * sinp) * q_scale
        q_out[:, out + LANE:out + 2 * LANE] = rot.astype(BF16)

    ckv = _rms(ckv_ref[...].astype(F32), kvn_ref[...])
    ckv_out[...] = ckv
    kp = kpe_ref[...].astype(F32)
    kpe = kp[:, :LANE] * cosp + kp[:, LANE:] * sinp
    kpe_out[...] = kpe[:, :ROPE]
    if with_kv:
        ckv_b = ckv.astype(BF16)
        kpe_b = kpe.astype(BF16)
        kn = _dot(ckv_b, wk_ref[...])
        for h in range(HEADS):
            lo = h * LANE
            k_out[:, 2 * lo:2 * lo + LANE] = kn[:, lo:lo + LANE].astype(BF16)
            k_out[:, 2 * lo + LANE:2 * lo + 2 * LANE] = kpe_b
        v_out[0] = _dot_nt(wvt_ref[...], ckv_b).astype(BF16)


Q_COLS = 3 * LANE


def mla_mid(proj, cos_t, sin_t, q_norm, kv_norm, wq, wk, wvt, *, tm, with_kv, name):
    m = proj.shape[0]
    nt = cos_t.shape[0] // tm
    hw = HEADS * LANE
    in_specs = [
        pl.BlockSpec((tm, LORA), lambda i: (i, MLA_GATE_W // LORA)),
        pl.BlockSpec((tm, LORA), lambda i: (i, MLA_GATE_W // LORA + 1)),
        pl.BlockSpec((tm, 2 * LANE), lambda i: (i, (MLA_GATE_W + 2 * LORA) // (2 * LANE))),
        pl.BlockSpec((tm, LANE), lambda i: (i % nt, 0)),
        pl.BlockSpec((tm, LANE), lambda i: (i % nt, 0)),
        _resident((1, LORA), lambda i: (0, 0)),
        _resident((1, LORA), lambda i: (0, 0)),
        _resident((LORA, HEADS * Q_COLS), lambda i: (0, 0)),
    ]
    args = [proj, proj, proj, cos_t, sin_t, q_norm.reshape(1, LORA), kv_norm.reshape(1, LORA), wq]
    out_specs = [pl.BlockSpec((tm, HEADS * HEAD_PAD), lambda i: (i, 0)),
                 pl.BlockSpec((tm, LORA), lambda i: (i, 0)),
                 pl.BlockSpec((tm, ROPE), lambda i: (i, 0))]
    out_shape = [jax.ShapeDtypeStruct((m, HEADS * HEAD_PAD), BF16),
                 jax.ShapeDtypeStruct((m, LORA), F32),
                 jax.ShapeDtypeStruct((m, ROPE), F32)]
    if with_kv:
        assert tm == VT_COLS
        in_specs += [_resident((LORA, hw), lambda i: (0, 0)), _resident((hw, LORA), lambda i: (0, 0))]
        args += [wk, wvt]
        out_specs += [pl.BlockSpec((tm, HEADS * HEAD_PAD), lambda i: (i, 0)),
                      pl.BlockSpec((1, hw, tm), lambda i: (i, 0, 0))]
        out_shape += [jax.ShapeDtypeStruct((m, HEADS * HEAD_PAD), BF16),
                      jax.ShapeDtypeStruct((m // tm, hw, tm), BF16)]
    return pl.pallas_call(
        functools.partial(_mla_mid_kernel, with_kv=with_kv),
        grid=(m // tm,),
        in_specs=in_specs, out_specs=out_specs, out_shape=out_shape,
        compiler_params=_cparams("parallel"),
        name=name,
    )(*args)


NEG_BIG = -1e30


FLASH_HEADS = 2


VT_COLS = 256


def _flash_kernel(q_ref, k_ref, vt_ref, o_ref, *, tile):
    qi = pl.program_id(2)
    slabs = tile // VT_COLS

    def step(ki, carry, masked):
        start = pl.multiple_of(ki * tile, tile)
        out = []
        for h in range(FLASH_HEADS):
            m_prev, l_prev, acc = carry[h]
            q = q_ref[:, h * HEAD_PAD:(h + 1) * HEAD_PAD]
            st = _dot_nt(k_ref[pl.ds(start, tile), h * HEAD_PAD:(h + 1) * HEAD_PAD], q)
            if masked:
                kc = lax.broadcasted_iota(jnp.int32, st.shape, 0) // CHUNK
                qc = lax.broadcasted_iota(jnp.int32, st.shape, 1) // CHUNK
                st = jnp.where(kc <= qc, st, NEG_BIG)
            m_new = jnp.maximum(m_prev, jnp.max(st, axis=0, keepdims=True))
            alpha = jnp.exp2(m_prev - m_new)
            p = jnp.exp2(st - m_new)
            l_new = alpha * l_prev + jnp.sum(p, axis=0, keepdims=True)
            p = p.astype(BF16)
            pv = None
            for j in range(slabs):
                vt = vt_ref[ki * slabs + j, h * NOPE:(h + 1) * NOPE, :]
                part = _dot(vt, p[j * VT_COLS:(j + 1) * VT_COLS, :])
                pv = part if pv is None else pv + part
            out.append((m_new, l_new, alpha * acc + pv))
        return tuple(out)

    init = tuple((jnp.full((1, tile), NEG_BIG, F32), jnp.zeros((1, tile), F32), jnp.zeros((NOPE, tile), F32))
                 for _ in range(FLASH_HEADS))
    carry = lax.fori_loop(0, qi, lambda ki, c: step(ki, c, False), init)
    carry = step(qi, carry, True)
    for h in range(FLASH_HEADS):
        _, l_fin, acc = carry[h]
        o_ref[:, h * NOPE:(h + 1) * NOPE] = (acc / l_fin).T.astype(o_ref.dtype)


def flash_attention(q, k, vt, *, batch, seq, tile, name):
    nq = seq // tile
    hq, hv = FLASH_HEADS * HEAD_PAD, FLASH_HEADS * NOPE
    return pl.pallas_call(
        functools.partial(_flash_kernel, tile=tile),
        grid=(batch, HEADS // FLASH_HEADS, nq),
        in_specs=[pl.BlockSpec((tile, hq), lambda b, h, i: (b * nq + i, h)),
                  pl.BlockSpec((seq, hq), lambda b, h, i: (b, h)),
                  pl.BlockSpec((seq // VT_COLS, hv, VT_COLS), lambda b, h, i: (b, h, 0))],
        out_specs=pl.BlockSpec((tile, hv), lambda b, h, i: (b * nq + i, h)),
        out_shape=jax.ShapeDtypeStruct((batch * seq, HEADS * NOPE), BF16),
        compiler_params=_cparams("parallel", "parallel", "arbitrary"),
        name=name,
    )(q, k, vt)


def _sample_attn_kernel(q_ref, cnew_ref, pnew_ref, ccache_ref, pcache_ref, wk_ref, wvt_ref, o_ref,
                        kc_scr, kp_scr, *, past, new, pad):
    total = past + pad
    kc_scr[0:past, :] = ccache_ref[0].astype(BF16)
    kp_scr[0:past, :] = pcache_ref[0].astype(BF16)
    kc_scr[past:total, :] = jnp.zeros((pad, LORA), BF16)
    kp_scr[past:total, :] = jnp.zeros((pad, ROPE), BF16)
    kc_scr[past:past + new, :] = cnew_ref[...].astype(BF16)
    kp_scr[past:past + new, :] = pnew_ref[...].astype(BF16)

    q = q_ref[...]
    qlat, qpe = [], []
    for h in range(HEADS):
        lo = h * HEAD_PAD
        qlat.append(_dot_nt(q[:, lo:lo + NOPE], wk_ref[:, h * NOPE:(h + 1) * NOPE]))
        qpe.append(q[:, lo + NOPE:lo + NOPE + ROPE])
    qlat = jnp.concatenate(qlat, axis=0).astype(BF16)
    qpe = jnp.concatenate(qpe, axis=0)
    kc = kc_scr[...]
    s = _dot_nt(qlat, kc) + _dot_nt(qpe, kp_scr[...])
    col = lax.broadcasted_iota(jnp.int32, s.shape, 1)
    s = jnp.where(col < past + new, s, NEG_BIG)
    m = jnp.max(s, axis=-1, keepdims=True)
    p = jnp.exp(s - m)
    l = jnp.sum(p, axis=-1, keepdims=True)
    olat = (_dot(p.astype(BF16), kc) / l).astype(BF16)
    for h in range(HEADS):
        o_ref[:, h * NOPE:(h + 1) * NOPE] = _dot_nt(
            olat[h * new:(h + 1) * new, :], wvt_ref[h * NOPE:(h + 1) * NOPE, :]).astype(o_ref.dtype)


def sample_attention(q, ckv_new, kpe_new, ckv_cache, kpe_cache, wk, wvt, *, name):
    nb, past, _ = ckv_cache.shape
    new = q.shape[0] // nb
    pad = LANE
    hw = HEADS * NOPE
    return pl.pallas_call(
        functools.partial(_sample_attn_kernel, past=past, new=new, pad=pad),
        grid=(nb,),
        in_specs=[pl.BlockSpec((new, HEADS * HEAD_PAD), lambda b: (b, 0)),
                  pl.BlockSpec((new, LORA), lambda b: (b, 0)),
                  pl.BlockSpec((new, ROPE), lambda b: (b, 0)),
                  pl.BlockSpec((1, past, LORA), lambda b: (b, 0, 0)),
                  pl.BlockSpec((1, past, ROPE), lambda b: (b, 0, 0)),
                  _resident((LORA, hw), lambda b: (0, 0)),
                  _resident((hw, LORA), lambda b: (0, 0))],
        out_specs=pl.BlockSpec((new, hw), lambda b: (b, 0)),
        out_shape=jax.ShapeDtypeStruct((nb * new, hw), BF16),
        scratch_shapes=[pltpu.VMEM((past + pad, LORA), BF16), pltpu.VMEM((past + pad, ROPE), BF16)],
        compiler_params=_cparams("arbitrary"),
        name=name,
    )(q, ckv_new, kpe_new, ckv_cache, kpe_cache, wk, wvt)


SUBLANES = 8
CONV_COLS = 512


def _conv_kernel(val_ref, glu_ref, hist_ref, dw_ref, dwb_ref, y_ref, tail_ref, ubuf, shifted, *, tt):
    i = pl.program_id(2)

    @pl.when(i == 0)
    def _():
        ubuf[0:CONV_HIST, :] = hist_ref[0]

    ubuf[CONV_HIST:CONV_HIST + tt, :] = val_ref[...].astype(F32) * _sigmoid(glu_ref[...].astype(F32))
    span = tt + CONV_HIST - SUBLANES
    for r in range(1, SUBLANES):
        shifted[r - 1] = ubuf[r:r + span, :]
    first = CONV_HIST - (CONV_K - 1)
    for c in range(0, ubuf.shape[1], LANE):
        acc = None
        for k in range(CONV_K):
            q, r = divmod(first + k, SUBLANES)
            rows = pl.ds(q * SUBLANES, tt)
            src = ubuf[rows, c:c + LANE] if r == 0 else shifted[r - 1, rows, c:c + LANE]
            term = src * dw_ref[k:k + 1, c:c + LANE]
            acc = term if acc is None else acc + term
        y_ref[:, c:c + LANE] = acc + dwb_ref[:, c:c + LANE]
    tail = ubuf[tt:tt + CONV_HIST, :]
    tail_ref[0] = tail
    ubuf[0:CONV_HIST, :] = tail


def conv_mixer(proj, hist, dw_w, dw_b, *, batch, seq, tt, name):
    d = dw_w.shape[1]
    nt, cw = seq // tt, CONV_COLS
    nc = d // cw
    return pl.pallas_call(
        functools.partial(_conv_kernel, tt=tt),
        grid=(batch, nc, nt),
        in_specs=[pl.BlockSpec((tt, cw), lambda b, j, i: (b * nt + i, j)),
                  pl.BlockSpec((tt, cw), lambda b, j, i: (b * nt + i, nc + j)),
                  pl.BlockSpec((1, CONV_HIST, cw), lambda b, j, i: (b, 0, j)),
                  pl.BlockSpec((CONV_K, cw), lambda b, j, i: (0, j)),
                  pl.BlockSpec((1, cw), lambda b, j, i: (0, j))],
        out_specs=[pl.BlockSpec((tt, cw), lambda b, j, i: (b * nt + i, j)),
                   pl.BlockSpec((1, CONV_HIST, cw), lambda b, j, i: (b, 0, j))],
        out_shape=[jax.ShapeDtypeStruct((batch * seq, d), F32),
                   jax.ShapeDtypeStruct((batch, CONV_HIST, d), F32)],
        scratch_shapes=[pltpu.VMEM((CONV_HIST + tt, cw), F32),
                        pltpu.VMEM((SUBLANES - 1, tt + CONV_HIST - SUBLANES, cw), F32)],
        compiler_params=_cparams("parallel", "parallel", "arbitrary"),
        name=name,
    )(proj, proj, hist, dw_w, dw_b.reshape(1, d))


def _softplus(x):
    return jnp.maximum(x, 0.0) + jnp.log1p(jnp.exp(-jnp.abs(x)))


def _ssd_kernel(z_ref, x_ref, b_ref, c_ref, dt_ref, hx_ref, hb_ref, hc_ref, st0_ref,
                cwx_ref, cwb_ref, cwc_ref, cbx_ref, cbb_ref, cbc_ref,
                dtal_ref, dskip_ref, gn_ref, expb_ref,
                y_ref, tx_ref, tb_ref, tc_ref, st_ref,
                xbuf, bbuf, cbuf, state, *, L, valid):
    ci = pl.program_id(2)

    @pl.when(ci == 0)
    def _():
        xbuf[0:SSM_HIST, :] = hx_ref[0]
        bbuf[0:SSM_HIST, :] = hb_ref[0]
        cbuf[0:SSM_HIST, :] = hc_ref[0]
        state[...] = st0_ref[0].T

    def short_conv(buf, new_ref, w_ref, bias_ref, tail_ref):
        new = new_ref[...]
        buf[SSM_HIST:SSM_HIST + L, :] = new.astype(F32)
        off = SSM_HIST - (SSM_CONV_K - 1)

        def direct(rows):
            acc = buf[off:off + rows, :] * w_ref[0:1, :]
            for k in range(1, SSM_CONV_K):
                acc = acc + buf[off + k:off + k + rows, :] * w_ref[k:k + 1, :]
            return acc

        if L % LANE == 0:
            last = SSM_CONV_K - 1
            cur = new.astype(F32)
            acc = cur * w_ref[last:last + 1, :]
            for j in range(1, SSM_CONV_K):
                acc = acc + pltpu.roll(cur, shift=j, axis=0) * w_ref[last - j:last - j + 1, :]
            acc = jnp.concatenate([direct(SUBLANES), acc[SUBLANES:, :]], axis=0)
        else:
            acc = direct(L)
        tail = buf[valid:valid + SSM_HIST, :]
        tail_ref[0] = tail
        buf[0:SSM_HIST, :] = tail
        return _silu(acc + bias_ref[...])

    x = short_conv(xbuf, x_ref, cwx_ref, cbx_ref, tx_ref)
    bm = short_conv(bbuf, b_ref, cwb_ref, cbb_ref, tb_ref)
    cm = short_conv(cbuf, c_ref, cwc_ref, cbc_ref, tc_ref)

    row = lax.broadcasted_iota(jnp.int32, (L, L), 0)
    col = lax.broadcasted_iota(jnp.int32, (L, L), 1)
    causal = col <= row
    dt = _softplus(dt_ref[...] + dtal_ref[0, 0:1, :])
    if valid < L:
        dt = jnp.where(lax.broadcasted_iota(jnp.int32, dt.shape, 0) < valid, dt, 0.0)
    da = dt * (-jnp.exp(dtal_ref[0, 1:2, :]))
    acs = _dot_onehot_lhs(causal.astype(BF16), da)
    acs_last = acs[L - 1:L, :]
    eacs = jnp.exp(acs)
    dte = jnp.exp(acs_last - acs) * dt
    acs_t = acs.T
    dt_t = dt.T

    eacs_x = _dot(eacs.astype(BF16), expb_ref[...])
    dte_x = _dot(dte.astype(BF16), expb_ref[...])
    cdec_x = _dot_onehot_rhs(jnp.broadcast_to(jnp.exp(acs_last), (SUBLANES, LANE)), expb_ref[...])[0:1, :]

    x_b = x.astype(BF16)
    bm_b = bm.astype(BF16)
    cm_b = cm.astype(BF16)
    cb = _dot_nt(cm_b, bm_b)
    st = state[...]
    y_off = _dot(cm_b, st.astype(BF16)) * eacs_x

    low_half = lax.broadcasted_iota(jnp.int32, (L, LANE), 1) < SSM_HEAD_DIM
    pairs = []
    for pr in range(SSM_GROUP_HEADS // 2):
        xp = x_b[:, pr * LANE:(pr + 1) * LANE]
        ys = []
        for e in (2 * pr, 2 * pr + 1):
            seg = jnp.where(causal, jnp.exp(acs[:, e:e + 1] - acs_t[e:e + 1, :]), 0.0)
            mm = (cb * seg * dt_t[e:e + 1, :]).astype(BF16)
            ys.append(_dot(mm, xp))
        pairs.append(jnp.where(low_half, ys[0], ys[1]))
    y = jnp.concatenate(pairs, axis=1) + y_off + dskip_ref[...] * x

    state[...] = st * cdec_x + _dot(bm.T.astype(BF16), (x * dte_x).astype(BF16))

    yz = y * _silu(z_ref[...].astype(F32))
    yn = (yz * lax.rsqrt(jnp.mean(yz * yz, axis=-1, keepdims=True) + EPS)) * gn_ref[...]
    y_ref[...] = yn.astype(y_ref.dtype)
    st_ref[0] = state[...].T


def ssd_mixer(proj, dtp, hist, st0, conv_w, conv_b, dtal, dskip, gnorm, *, batch, nchunks, L, valid, name):
    gw, ns = SSM_GROUP_W, D_STATE
    xb = (SSM_GROUPS * gw) // gw
    bb = (2 * SSM_GROUPS * gw) // ns
    cb = bb + SSM_GROUPS
    pb = (SSM_GROUPS * gw) // ns
    pc = pb + SSM_GROUPS
    rows = batch * nchunks * L
    rmap = lambda off: (lambda b, g, c: (b * nchunks + c, off + g))
    hmap = lambda off: (lambda b, g, c: (b, 0, off + g))
    pmap = lambda off: (lambda b, g, c: (0, off + g))
    in_specs = [
        pl.BlockSpec((L, gw), rmap(0)), pl.BlockSpec((L, gw), rmap(xb)),
        pl.BlockSpec((L, ns), rmap(bb)), pl.BlockSpec((L, ns), rmap(cb)),
        pl.BlockSpec((L, LANE), rmap(0)),
        pl.BlockSpec((1, SSM_HIST, gw), hmap(0)), pl.BlockSpec((1, SSM_HIST, ns), hmap(pb)),
        pl.BlockSpec((1, SSM_HIST, ns), hmap(pc)),
        pl.BlockSpec((1, gw, ns), lambda b, g, c: (b, g, 0)),
        pl.BlockSpec((SSM_CONV_K, gw), pmap(0)), pl.BlockSpec((SSM_CONV_K, ns), pmap(pb)),
        pl.BlockSpec((SSM_CONV_K, ns), pmap(pc)),
        pl.BlockSpec((1, gw), pmap(0)), pl.BlockSpec((1, ns), pmap(pb)), pl.BlockSpec((1, ns), pmap(pc)),
        pl.BlockSpec((1, 2, LANE), lambda b, g, c: (g, 0, 0)),
        pl.BlockSpec((1, gw), pmap(0)), pl.BlockSpec((1, gw), pmap(0)),
        _resident((LANE, gw), lambda b, g, c: (0, 0)),
    ]
    expand = (np.arange(gw)[None, :] // SSM_HEAD_DIM) == np.arange(LANE)[:, None]
    out_specs = [
        pl.BlockSpec((L, gw), rmap(0)),
        pl.BlockSpec((1, SSM_HIST, gw), hmap(0)), pl.BlockSpec((1, SSM_HIST, ns), hmap(0)),
        pl.BlockSpec((1, SSM_HIST, ns), hmap(0)),
        pl.BlockSpec((1, gw, ns), lambda b, g, c: (b, g, 0)),
    ]
    out_shape = [
        jax.ShapeDtypeStruct((rows, SSM_GROUPS * gw), BF16),
        jax.ShapeDtypeStruct((batch, SSM_HIST, SSM_GROUPS * gw), F32),
        jax.ShapeDtypeStruct((batch, SSM_HIST, SSM_GROUPS * ns), F32),
        jax.ShapeDtypeStruct((batch, SSM_HIST, SSM_GROUPS * ns), F32),
        jax.ShapeDtypeStruct((batch, SSM_GROUPS * gw, ns), F32),
    ]
    return pl.pallas_call(
        functools.partial(_ssd_kernel, L=L, valid=valid),
        grid=(batch, SSM_GROUPS, nchunks),
        in_specs=in_specs, out_specs=out_specs, out_shape=out_shape,
        scratch_shapes=[pltpu.VMEM((SSM_HIST + L, gw), F32), pltpu.VMEM((SSM_HIST + L, ns), F32),
                        pltpu.VMEM((SSM_HIST + L, ns), F32), pltpu.VMEM((ns, gw), F32)],
        compiler_params=_cparams("parallel", "parallel", "arbitrary"),
        name=name,
    )(proj, proj, proj, proj, dtp, hist, hist, hist, st0,
      conv_w, conv_w, conv_w, conv_b, conv_b, conv_b, dtal, dskip, gnorm, jnp.asarray(expand, BF16))


def _rope_tables(pos):
    half = ROPE // 2
    freqs = np.float32(ROPE_THETA) ** (-np.arange(half, dtype=np.float32) / np.float32(half))
    ang = pos.astype(np.float32)[:, None] * freqs[None, :]
    zeros = np.zeros((pos.shape[0], LANE - ROPE), np.float32)
    cos, sin = np.cos(ang), np.sin(ang)
    return np.concatenate([cos, cos, zeros], axis=1), np.concatenate([sin, sin, zeros], axis=1)


def _rot_cols(w):
    half = ROPE // 2
    return jnp.concatenate([-w[..., half:], w[..., :half]], axis=-1)


def _mla_weights(w_in, w_qb, w_kvb, w_out):
    d = w_in.shape[0]
    qa, ckv = w_in[:, :LORA], w_in[:, LORA:2 * LORA]
    kpe, gate = w_in[:, 2 * LORA:2 * LORA + ROPE], w_in[:, 2 * LORA + ROPE:]
    z = jnp.zeros((d, LANE - ROPE), F32)
    w0 = jnp.concatenate([gate, qa, ckv, kpe, z, _rot_cols(kpe), z], axis=1).astype(BF16)
    hw = HEADS * NOPE
    r = w_qb[:, :, NOPE:]
    zr = jnp.zeros((LORA, HEADS, LANE - ROPE), F32)
    wq = jnp.concatenate([w_qb[:, :, :NOPE], r, zr, _rot_cols(r), zr], axis=-1)
    return dict(
        w0=w0,
        wq=wq.reshape(LORA, HEADS * Q_COLS).astype(BF16),
        wk=w_kvb[:, :, :NOPE].reshape(LORA, hw).astype(BF16),
        wvt=w_kvb[:, :, NOPE:].reshape(LORA, hw).T.astype(BF16),
        wo=w_out.astype(BF16),
    )


def _mla_layer(xp, xs, norm_g, q_norm, kv_norm, wts, ckv_cache, kpe_cache, tabs_p, tabs_s,
               *, batch, seq, fin, tag):
    mid = functools.partial(mla_mid, q_norm=q_norm, kv_norm=kv_norm, wq=wts["wq"], wk=wts["wk"], wvt=wts["wvt"])
    proj_p = norm_matmul(xp, norm_g, wts["w0"], tm=512, tn=MLA_PROJ_W, out_dtype=BF16, name=tag + "_inproj_p")
    q, ckv_p, kpe_p, k, vt = mid(proj_p, *tabs_p, tm=VT_COLS, with_kv=True, name=tag + "_mid_p")
    o = flash_attention(q, k, vt, batch=batch, seq=seq, tile=min(seq, 1024), name=tag + "_flash")
    xp = out_proj(o, proj_p, wts["wo"], xp, fin, tm=256, name=tag + "_outproj_p")

    ms = xs.shape[0]
    proj_s = norm_matmul(xs, norm_g, wts["w0"], tm=ms, tn=MLA_PROJ_W, out_dtype=BF16, name=tag + "_inproj_s")
    qs, ckv_s, kpe_s = mid(proj_s, *tabs_s, tm=ms, with_kv=False, name=tag + "_mid_s")
    os_ = sample_attention(qs, ckv_s, kpe_s, ckv_cache, kpe_cache, wts["wk"], wts["wvt"], name=tag + "_attn_s")
    xs = out_proj(os_, proj_s, wts["wo"], xs, fin, tm=ms, name=tag + "_outproj_s")
    return xp, xs, ckv_p, kpe_p, ckv_s, kpe_s


def _ssd_layer(xp, xs, l2_norm, l2_w_in, l2_conv_w, l2_conv_b, l2_dt_bias, l2_a_log, l2_d_skip, l2_gnorm, l2_w_out,
               state_l2_conv, state_l2_ssm, *, batch, seq, nb, new, chunk_p=256, chunk_s=None):
    d = xp.shape[1]
    ms = nb * new
    inner = SSM_GROUPS * SSM_GROUP_W
    bc = SSM_GROUPS * D_STATE
    main_w = 2 * inner + 2 * bc
    w2 = l2_w_in.astype(BF16)
    w2dt = jnp.pad(l2_w_in[:, main_w:].reshape(d, SSM_GROUPS, SSM_GROUP_HEADS),
                   ((0, 0), (0, 0), (0, LANE - SSM_GROUP_HEADS))).reshape(d, SSM_GROUPS * LANE).astype(BF16)
    w2o = l2_w_out.astype(BF16)
    dtal = jnp.pad(jnp.stack([l2_dt_bias, l2_a_log]).reshape(2, SSM_GROUPS, SSM_GROUP_HEADS).transpose(1, 0, 2),
                   ((0, 0), (0, 0), (0, LANE - SSM_GROUP_HEADS)))
    dskip = jnp.repeat(l2_d_skip, SSM_HEAD_DIM).reshape(1, inner)
    hist_rows = SSM_CONV_K - 1

    def run_ssd(proj, dtp, hist, st0, nbatch, nchunks, L, valid, name):
        hist = jnp.pad(hist, ((0, 0), (SSM_HIST - hist_rows, 0), (0, 0)))
        y, tx, tb, tc, st = ssd_mixer(
            proj, dtp, hist, st0.reshape(nbatch, inner, D_STATE), l2_conv_w, l2_conv_b.reshape(1, -1), dtal, dskip,
            l2_gnorm.reshape(1, inner), batch=nbatch, nchunks=nchunks, L=L, valid=valid, name=name)
        tail = jnp.concatenate([tx, tb, tc], axis=-1)[:, SSM_HIST - hist_rows:]
        return y, tail, st.reshape(nbatch, SSM_GROUPS * SSM_GROUP_HEADS, SSM_HEAD_DIM, D_STATE)

    proj_p = norm_matmul(xp, l2_norm, w2, tm=256, tn=main_w // 2, n_out=main_w, out_dtype=BF16, name="l2_inproj_p")
    dt_p = norm_matmul(xp, l2_norm, w2dt, tm=512, tn=SSM_GROUPS * LANE, out_dtype=F32, name="l2_dtproj_p")
    y_p, l2_conv_p, l2_ssm_p = run_ssd(
        proj_p, dt_p, jnp.zeros((batch, hist_rows, inner + 2 * bc), F32),
        jnp.zeros((batch, inner, D_STATE), F32), batch, seq // chunk_p, chunk_p, chunk_p, "l2_ssd_p")
    xp = out_proj(y_p, None, w2o, xp, None, tm=256, name="l2_outproj_p")

    proj_s = norm_matmul(xs, l2_norm, w2, tm=ms, tn=main_w // 2, n_out=main_w, out_dtype=BF16, name="l2_inproj_s")
    dt_s = norm_matmul(xs, l2_norm, w2dt, tm=ms, tn=SSM_GROUPS * LANE, out_dtype=F32, name="l2_dtproj_s")
    L = chunk_s if chunk_s is not None else -(-new // SUBLANES) * SUBLANES
    pad_rows = lambda a: jnp.pad(a.reshape(nb, new, a.shape[1]), ((0, 0), (0, L - new), (0, 0))).reshape(nb * L, a.shape[1])
    y_s, l2_conv_s, l2_ssm_s = run_ssd(pad_rows(proj_s), pad_rows(dt_s), state_l2_conv, state_l2_ssm,
                                       nb, 1, L, new, "l2_ssd_s")
    y_s = y_s.reshape(nb, L, inner)[:, :new].reshape(ms, inner)
    xs = out_proj(y_s, None, w2o, xs, None, tm=ms, name="l2_outproj_s")

    return xp, xs, l2_conv_p, l2_ssm_p, l2_conv_s, l2_ssm_s


def kernel(x_prompt, x_sample, cache_l0_ckv, cache_l0_kpe, state_l1_conv, state_l2_conv, state_l2_ssm, cache_l3_ckv, cache_l3_kpe, l0_norm, l0_w_in, l0_q_norm, l0_w_qb, l0_kv_norm, l0_w_kvb, l0_w_out, l1_norm, l1_w_in, l1_dw_w, l1_dw_b, l1_ln_g, l1_ln_b, l1_w_out, l2_norm, l2_w_in, l2_conv_w, l2_conv_b, l2_dt_bias, l2_a_log, l2_d_skip, l2_gnorm, l2_w_out, l3_norm, l3_w_in, l3_q_norm, l3_w_qb, l3_kv_norm, l3_w_kvb, l3_w_out, final_norm):
    batch, seq, d = x_prompt.shape
    nb, new, _ = x_sample.shape
    past = cache_l0_ckv.shape[1]
    xp = x_prompt.reshape(batch * seq, d)
    xs = x_sample.reshape(nb * new, d)
    ms = nb * new

    tabs_p = _rope_tables(np.arange(seq))
    cs, sn = _rope_tables(past + np.arange(new))
    tabs_s = (np.tile(cs, (nb, 1)), np.tile(sn, (nb, 1)))

    w0 = _mla_weights(l0_w_in, l0_w_qb, l0_w_kvb, l0_w_out)
    xp, xs, l0_ckv_p, l0_kpe_p, l0_ckv_s, l0_kpe_s = _mla_layer(
        xp, xs, l0_norm, l0_q_norm, l0_kv_norm, w0, cache_l0_ckv, cache_l0_kpe, tabs_p, tabs_s,
        batch=batch, seq=seq, fin=None, tag="l0")

    w1 = l1_w_in.astype(BF16)
    w1o = l1_w_out.astype(BF16)
    pad_hist = lambda h: jnp.pad(h, ((0, 0), (CONV_HIST - (CONV_K - 1), 0), (0, 0)))
    proj_p = norm_matmul(xp, l1_norm, w1, tm=512, tn=w1.shape[1] // 2, out_dtype=BF16, name="l1_inproj_p")
    ln1 = (l1_ln_g, l1_ln_b)
    gate_block = 2
    c_p, tail_p = conv_mixer(proj_p, jnp.zeros((batch, CONV_HIST, d), F32), l1_dw_w, l1_dw_b,
                             batch=batch, seq=seq, tt=256, name="l1_conv_p")
    xp = out_proj(c_p, proj_p, w1o, xp, None, tm=256, gate_block=gate_block, ln=ln1, name="l1_outproj_p")
    proj_s = norm_matmul(xs, l1_norm, w1, tm=ms, tn=w1.shape[1] // 2, out_dtype=BF16, name="l1_inproj_s")
    c_s, tail_s = conv_mixer(proj_s, pad_hist(state_l1_conv), l1_dw_w, l1_dw_b,
                             batch=nb, seq=new, tt=new, name="l1_conv_s")
    xs = out_proj(c_s, proj_s, w1o, xs, None, tm=ms, gate_block=gate_block, ln=ln1, name="l1_outproj_s")
    l1_conv_p = tail_p[:, CONV_HIST - (CONV_K - 1):]
    l1_conv_s = tail_s[:, CONV_HIST - (CONV_K - 1):]

    xp, xs, l2_conv_p, l2_ssm_p, l2_conv_s, l2_ssm_s = _ssd_layer(
        xp, xs, l2_norm, l2_w_in, l2_conv_w, l2_conv_b, l2_dt_bias, l2_a_log, l2_d_skip, l2_gnorm, l2_w_out,
        state_l2_conv, state_l2_ssm, batch=batch, seq=seq, nb=nb, new=new)

    w3 = _mla_weights(l3_w_in, l3_w_qb, l3_w_kvb, l3_w_out)
    yp, ys, l3_ckv_p, l3_kpe_p, l3_ckv_s, l3_kpe_s = _mla_layer(
        xp, xs, l3_norm, l3_q_norm, l3_kv_norm, w3, cache_l3_ckv, cache_l3_kpe, tabs_p, tabs_s,
        batch=batch, seq=seq, fin=final_norm, tag="l3")

    r3 = lambda a, n: a.reshape(n, -1, a.shape[-1])
    return (yp.reshape(batch, seq, d), ys.reshape(nb, new, d),
            r3(l0_ckv_p, batch), r3(l0_kpe_p, batch), r3(l0_ckv_s, nb), r3(l0_kpe_s, nb),
            l1_conv_p, l1_conv_s,
            l2_conv_p, l2_ssm_p, l2_conv_s, l2_ssm_s,
            r3(l3_ckv_p, batch), r3(l3_kpe_p, batch), r3(l3_ckv_s, nb), r3(l3_kpe_s, nb))
```

```python
import functools

import jax
import jax.numpy as jnp
import numpy as np
from jax import lax
from jax.experimental import pallas as pl
from jax.experimental.pallas import tpu as pltpu

F32 = jnp.float32
BF16 = jnp.bfloat16

EPS = 1e-6
ROPE_THETA = 10000.0
CHUNK = 64
HEADS = 16
NOPE = 128
ROPE = 64
HEAD_PAD = 256
MLA_SCALE = (NOPE + ROPE) ** -0.5
LOG2E = 1.4426950408889634
LORA = 512
CONV_K = 31
CONV_HIST = 32
SSM_GROUPS = 8
SSM_GROUP_HEADS = 8
SSM_HEAD_DIM = 64
SSM_GROUP_W = SSM_GROUP_HEADS * SSM_HEAD_DIM
D_STATE = 128
SSM_CONV_K = 4
SSM_HIST = 8
LANE = 128

VMEM_LIMIT_BYTES = 56 * 1024 * 1024


def _cparams(*sem):
    return pltpu.CompilerParams(dimension_semantics=sem, vmem_limit_bytes=VMEM_LIMIT_BYTES)


def _resident(shape, index_map):
    return pl.BlockSpec(shape, index_map, pipeline_mode=pl.Buffered(1))


def _sigmoid(x):
    return 0.5 * jnp.tanh(0.5 * x) + 0.5


def _silu(x):
    h = 0.5 * x
    return h + h * jnp.tanh(h)


def _split3(x):
    hi = x.astype(BF16)
    r = x - hi.astype(F32)
    mid = r.astype(BF16)
    return hi, mid, (r - mid.astype(F32)).astype(BF16)


def _dot_onehot_lhs(sel, x):
    hi, mid, lo = _split3(x)
    return _dot(sel, hi) + (_dot(sel, mid) + _dot(sel, lo))


def _dot_onehot_rhs(x, sel):
    hi, mid, lo = _split3(x)
    return _dot(hi, sel) + (_dot(mid, sel) + _dot(lo, sel))


def _rms(x, g):
    return (x * lax.rsqrt(jnp.mean(x * x, axis=-1, keepdims=True) + EPS)) * g


def _dot(a, b):
    return jnp.dot(a, b, preferred_element_type=F32)


def _dot_nt(a, b):
    return lax.dot_general(a, b, (((1,), (1,)), ((), ())), preferred_element_type=F32)


def _norm_matmul_kernel(x_ref, g_ref, w_ref, o_ref, *, col_chunk):
    xn = _rms(x_ref[...], g_ref[...]).astype(BF16)
    for c in range(0, o_ref.shape[1], col_chunk):
        o_ref[:, c:c + col_chunk] = _dot(xn, w_ref[:, c:c + col_chunk]).astype(o_ref.dtype)


def norm_matmul(x, g, w, *, tm, tn, out_dtype, name, n_out=None):
    m, k = x.shape
    n = w.shape[1] if n_out is None else n_out
    col_chunk = 256 if tn % 256 == 0 else LANE
    return pl.pallas_call(
        functools.partial(_norm_matmul_kernel, col_chunk=col_chunk),
        grid=(n // tn, m // tm),
        in_specs=[pl.BlockSpec((tm, k), lambda j, i: (i, 0)),
                  _resident((1, k), lambda j, i: (0, 0)),
                  _resident((k, tn), lambda j, i: (0, j))],
        out_specs=pl.BlockSpec((tm, tn), lambda j, i: (i, j)),
        out_shape=jax.ShapeDtypeStruct((m, n), out_dtype),
        compiler_params=_cparams("arbitrary", "arbitrary"),
        name=name,
    )(x, g.reshape(1, k), w)


def _out_proj_kernel(*refs, gated, layer_norm, final):
    refs = list(refs)
    a_ref = refs.pop(0)
    gate_ref = refs.pop(0) if gated else None
    lng_ref, lnb_ref = (refs.pop(0), refs.pop(0)) if layer_norm else (None, None)
    w_ref, res_ref = refs.pop(0), refs.pop(0)
    fin_ref = refs.pop(0) if final else None
    o_ref = refs.pop(0)
    a = a_ref[...]
    if layer_norm:
        a = a.astype(F32)
        cen = a - jnp.mean(a, axis=-1, keepdims=True)
        var = jnp.mean(cen * cen, axis=-1, keepdims=True)
        a = _silu((cen * lax.rsqrt(var + EPS)) * lng_ref[...] + lnb_ref[...])
    if gated:
        a = a.astype(F32) * _silu(gate_ref[...].astype(F32))
    y = res_ref[...] + _dot(a.astype(BF16), w_ref[...])
    if final:
        y = _rms(y, fin_ref[...])
    o_ref[...] = y


def out_proj(a, gate_src, w, res, fin, *, tm, name, gate_block=0, ln=None):
    m, k = a.shape
    n = w.shape[1]
    gated, final, layer_norm = gate_src is not None, fin is not None, ln is not None
    in_specs = [pl.BlockSpec((tm, k), lambda i: (i, 0))]
    args = [a]
    if gated:
        in_specs.append(pl.BlockSpec((tm, k), lambda i: (i, gate_block)))
        args.append(gate_src)
    if layer_norm:
        in_specs += [_resident((1, k), lambda i: (0, 0)), _resident((1, k), lambda i: (0, 0))]
        args += [ln[0].reshape(1, k), ln[1].reshape(1, k)]
    in_specs += [_resident((k, n), lambda i: (0, 0)), pl.BlockSpec((tm, n), lambda i: (i, 0))]
    args += [w, res]
    if final:
        in_specs.append(_resident((1, n), lambda i: (0, 0)))
        args.append(fin.reshape(1, n))
    return pl.pallas_call(
        functools.partial(_out_proj_kernel, gated=gated, layer_norm=layer_norm, final=final),
        grid=(m // tm,),
        in_specs=in_specs,
        out_specs=pl.BlockSpec((tm, n), lambda i: (i, 0)),
        out_shape=jax.ShapeDtypeStruct((m, n), F32),
        compiler_params=_cparams("parallel"),
        name=name,
    )(*args)


MLA_GATE_W = HEADS * NOPE
MLA_PROJ_W = MLA_GATE_W + 2 * LORA + 2 * LANE


def _mla_mid_kernel(*refs, with_kv):
    qa_ref, ckv_ref, kpe_ref, cos_ref, sin_ref, qn_ref, kvn_ref, wq_ref = refs[:8]
    refs = refs[8:]
    if with_kv:
        wk_ref, wvt_ref, q_out, ckv_out, kpe_out, k_out, v_out = refs
    else:
        q_out, ckv_out, kpe_out = refs
    cosp, sinp = cos_ref[...], sin_ref[...]
    q_scale = MLA_SCALE * LOG2E if with_kv else MLA_SCALE

    qa = _rms(qa_ref[...].astype(F32), qn_ref[...]).astype(BF16)
    qall = _dot(qa, wq_ref[...])
    for h in range(HEADS):
        lo, out = h * Q_COLS, h * HEAD_PAD
        q_out[:, out:out + LANE] = (qall[:, lo:lo + LANE] * q_scale).astype(BF16)
        pair = qall[:, lo + LANE:lo + 2 * LANE]
        rot = (pair * cosp + pltpu.roll(pair, ROPE, axis=1) * sinp) * q_scale
        q_out[:, out + LANE:out + 2 * LANE] = rot.astype(BF16)

    ckv = _rms(ckv_ref[...].astype(F32), kvn_ref[...])
    ckv_out[...] = ckv
    kp = kpe_ref[...].astype(F32)
    kpe = kp[:, :LANE] * cosp + kp[:, LANE:] * sinp
    kpe_out[...] = kpe[:, :ROPE]
    if with_kv:
        ckv_b = ckv.astype(BF16)
        kpe_b = kpe.astype(BF16)
        kn = _dot(ckv_b, wk_ref[...])
        for h in range(HEADS):
            lo = h * LANE
            k_out[:, 2 * lo:2 * lo + LANE] = kn[:, lo:lo + LANE].astype(BF16)
            k_out[:, 2 * lo + LANE:2 * lo + 2 * LANE] = kpe_b
        v_out[0] = _dot_nt(wvt_ref[...], ckv_b).astype(BF16)


Q_COLS = 2 * LANE


def mla_mid(proj, cos_t, sin_t, q_norm, kv_norm, wq, wk, wvt, *, tm, with_kv, name):
    m = proj.shape[0]
    nt = cos_t.shape[0] // tm
    hw = HEADS * LANE
    in_specs = [
        pl.BlockSpec((tm, LORA), lambda i: (i, MLA_GATE_W // LORA)),
        pl.BlockSpec((tm, LORA), lambda i: (i, MLA_GATE_W // LORA + 1)),
        pl.BlockSpec((tm, 2 * LANE), lambda i: (i, (MLA_GATE_W + 2 * LORA) // (2 * LANE))),
        pl.BlockSpec((tm, LANE), lambda i: (i % nt, 0)),
        pl.BlockSpec((tm, LANE), lambda i: (i % nt, 0)),
        _resident((1, LORA), lambda i: (0, 0)),
        _resident((1, LORA), lambda i: (0, 0)),
        _resident((LORA, HEADS * Q_COLS), lambda i: (0, 0)),
    ]
    args = [proj, proj, proj, cos_t, sin_t, q_norm.reshape(1, LORA), kv_norm.reshape(1, LORA), wq]
    out_specs = [pl.BlockSpec((tm, HEADS * HEAD_PAD), lambda i: (i, 0)),
                 pl.BlockSpec((tm, LORA), lambda i: (i, 0)),
                 pl.BlockSpec((tm, ROPE), lambda i: (i, 0))]
    out_shape = [jax.ShapeDtypeStruct((m, HEADS * HEAD_PAD), BF16),
                 jax.ShapeDtypeStruct((m, LORA), F32),
                 jax.ShapeDtypeStruct((m, ROPE), F32)]
    if with_kv:
        assert tm == VT_COLS
        in_specs += [_resident((LORA, hw), lambda i: (0, 0)), _resident((hw, LORA), lambda i: (0, 0))]
        args += [wk, wvt]
        out_specs += [pl.BlockSpec((tm, HEADS * HEAD_PAD), lambda i: (i, 0)),
                      pl.BlockSpec((1, hw, tm), lambda i: (i, 0, 0))]
        out_shape += [jax.ShapeDtypeStruct((m, HEADS * HEAD_PAD), BF16),
                      jax.ShapeDtypeStruct((m // tm, hw, tm), BF16)]
    return pl.pallas_call(
        functools.partial(_mla_mid_kernel, with_kv=with_kv),
        grid=(m // tm,),
        in_specs=in_specs, out_specs=out_specs, out_shape=out_shape,
        compiler_params=_cparams("parallel"),
        name=name,
    )(*args)


NEG_BIG = -1e30


FLASH_HEADS = 2


VT_COLS = 256


def _flash_kernel(q_ref, k_ref, vt_ref, o_ref, *, tile):
    qi = pl.program_id(2)
    slabs = tile // VT_COLS

    def step(ki, carry, masked):
        start = pl.multiple_of(ki * tile, tile)
        out = []
        for h in range(FLASH_HEADS):
            m_prev, l_prev, acc = carry[h]
            q = q_ref[:, h * HEAD_PAD:(h + 1) * HEAD_PAD]
            st = _dot_nt(k_ref[pl.ds(start, tile), h * HEAD_PAD:(h + 1) * HEAD_PAD], q)
            if masked:
                kc = lax.broadcasted_iota(jnp.int32, st.shape, 0) // CHUNK
                qc = lax.broadcasted_iota(jnp.int32, st.shape, 1) // CHUNK
                st = jnp.where(kc <= qc, st, NEG_BIG)
            m_new = jnp.maximum(m_prev, jnp.max(st, axis=0, keepdims=True))
            alpha = jnp.exp2(m_prev - m_new)
            p = jnp.exp2(st - m_new)
            l_new = alpha * l_prev + jnp.sum(p, axis=0, keepdims=True)
            p = p.astype(BF16)
            pv = None
            for j in range(slabs):
                vt = vt_ref[ki * slabs + j, h * NOPE:(h + 1) * NOPE, :]
                part = _dot(vt, p[j * VT_COLS:(j + 1) * VT_COLS, :])
                pv = part if pv is None else pv + part
            out.append((m_new, l_new, alpha * acc + pv))
        return tuple(out)

    init = tuple((jnp.full((1, tile), NEG_BIG, F32), jnp.zeros((1, tile), F32), jnp.zeros((NOPE, tile), F32))
                 for _ in range(FLASH_HEADS))
    carry = lax.fori_loop(0, qi, lambda ki, c: step(ki, c, False), init)
    carry = step(qi, carry, True)
    for h in range(FLASH_HEADS):
        _, l_fin, acc = carry[h]
        o_ref[:, h * NOPE:(h + 1) * NOPE] = (acc / l_fin).T.astype(o_ref.dtype)


def flash_attention(q, k, vt, *, batch, seq, tile, name):
    nq = seq // tile
    hq, hv = FLASH_HEADS * HEAD_PAD, FLASH_HEADS * NOPE
    return pl.pallas_call(
        functools.partial(_flash_kernel, tile=tile),
        grid=(batch, HEADS // FLASH_HEADS, nq),
        in_specs=[pl.BlockSpec((tile, hq), lambda b, h, i: (b * nq + i, h)),
                  pl.BlockSpec((seq, hq), lambda b, h, i: (b, h)),
                  pl.BlockSpec((seq // VT_COLS, hv, VT_COLS), lambda b, h, i: (b, h, 0))],
        out_specs=pl.BlockSpec((tile, hv), lambda b, h, i: (b * nq + i, h)),
        out_shape=jax.ShapeDtypeStruct((batch * seq, HEADS * NOPE), BF16),
        compiler_params=_cparams("parallel", "parallel", "arbitrary"),
        name=name,
    )(q, k, vt)


def _sample_attn_kernel(q_ref, cnew_ref, pnew_ref, ccache_ref, pcache_ref, wk_ref, wvt_ref, o_ref,
                        kc_scr, kp_scr, *, past, new, pad):
    total = past + pad
    kc_scr[0:past, :] = ccache_ref[0].astype(BF16)
    kp_scr[0:past, :] = pcache_ref[0].astype(BF16)
    kc_scr[past:total, :] = jnp.zeros((pad, LORA), BF16)
    kp_scr[past:total, :] = jnp.zeros((pad, ROPE), BF16)
    kc_scr[past:past + new, :] = cnew_ref[...].astype(BF16)
    kp_scr[past:past + new, :] = pnew_ref[...].astype(BF16)

    q = q_ref[...]
    qlat, qpe = [], []
    for h in range(HEADS):
        lo = h * HEAD_PAD
        qlat.append(_dot_nt(q[:, lo:lo + NOPE], wk_ref[:, h * NOPE:(h + 1) * NOPE]))
        qpe.append(q[:, lo + NOPE:lo + NOPE + ROPE])
    qlat = jnp.concatenate(qlat, axis=0).astype(BF16)
    qpe = jnp.concatenate(qpe, axis=0)
    kc = kc_scr[...]
    s = _dot_nt(qlat, kc) + _dot_nt(qpe, kp_scr[...])
    col = lax.broadcasted_iota(jnp.int32, s.shape, 1)
    s = jnp.where(col < past + new, s, NEG_BIG)
    m = jnp.max(s, axis=-1, keepdims=True)
    p = jnp.exp(s - m)
    l = jnp.sum(p, axis=-1, keepdims=True)
    olat = (_dot(p.astype(BF16), kc) / l).astype(BF16)
    for h in range(HEADS):
        o_ref[:, h * NOPE:(h + 1) * NOPE] = _dot_nt(
            olat[h * new:(h + 1) * new, :], wvt_ref[h * NOPE:(h + 1) * NOPE, :]).astype(o_ref.dtype)


def sample_attention(q, ckv_new, kpe_new, ckv_cache, kpe_cache, wk, wvt, *, name):
    nb, past, _ = ckv_cache.shape
    new = q.shape[0] // nb
    pad = LANE
    hw = HEADS * NOPE
    return pl.pallas_call(
        functools.partial(_sample_attn_kernel, past=past, new=new, pad=pad),
        grid=(nb,),
        in_specs=[pl.BlockSpec((new, HEADS * HEAD_PAD), lambda b: (b, 0)),
                  pl.BlockSpec((new, LORA), lambda b: (b, 0)),
                  pl.BlockSpec((new, ROPE), lambda b: (b, 0)),
                  pl.BlockSpec((1, past, LORA), lambda b: (b, 0, 0)),
                  pl.BlockSpec((1, past, ROPE), lambda b: (b, 0, 0)),
                  _resident((LORA, hw), lambda b: (0, 0)),
                  _resident((hw, LORA), lambda b: (0, 0))],
        out_specs=pl.BlockSpec((new, hw), lambda b: (b, 0)),
        out_shape=jax.ShapeDtypeStruct((nb * new, hw), BF16),
        scratch_shapes=[pltpu.VMEM((past + pad, LORA), BF16), pltpu.VMEM((past + pad, ROPE), BF16)],
        compiler_params=_cparams("arbitrary"),
        name=name,
    )(q, ckv_new, kpe_new, ckv_cache, kpe_cache, wk, wvt)


SUBLANES = 8
CONV_COLS = 512


def _conv_kernel(val_ref, glu_ref, hist_ref, dw_ref, dwb_ref, y_ref, tail_ref, ubuf, shifted, *, tt):
    i = pl.program_id(2)

    @pl.when(i == 0)
    def _():
        ubuf[0:CONV_HIST, :] = hist_ref[0]

    ubuf[CONV_HIST:CONV_HIST + tt, :] = val_ref[...].astype(F32) * _sigmoid(glu_ref[...].astype(F32))
    span = tt + CONV_HIST - SUBLANES
    for r in range(1, SUBLANES):
        shifted[r - 1] = ubuf[r:r + span, :]
    first = CONV_HIST - (CONV_K - 1)
    for c in range(0, ubuf.shape[1], LANE):
        acc = None
        for k in range(CONV_K):
            q, r = divmod(first + k, SUBLANES)
            rows = pl.ds(q * SUBLANES, tt)
            src = ubuf[rows, c:c + LANE] if r == 0 else shifted[r - 1, rows, c:c + LANE]
            term = src * dw_ref[k:k + 1, c:c + LANE]
            acc = term if acc is None else acc + term
        y_ref[:, c:c + LANE] = (acc + dwb_ref[:, c:c + LANE]).astype(y_ref.dtype)
    tail = ubuf[tt:tt + CONV_HIST, :]
    tail_ref[0] = tail
    ubuf[0:CONV_HIST, :] = tail


def conv_mixer(proj, hist, dw_w, dw_b, *, batch, seq, tt, name):
    d = dw_w.shape[1]
    nt, cw = seq // tt, CONV_COLS
    nc = d // cw
    return pl.pallas_call(
        functools.partial(_conv_kernel, tt=tt),
        grid=(batch, nc, nt),
        in_specs=[pl.BlockSpec((tt, cw), lambda b, j, i: (b * nt + i, j)),
                  pl.BlockSpec((tt, cw), lambda b, j, i: (b * nt + i, nc + j)),
                  pl.BlockSpec((1, CONV_HIST, cw), lambda b, j, i: (b, 0, j)),
                  pl.BlockSpec((CONV_K, cw), lambda b, j, i: (0, j)),
                  pl.BlockSpec((1, cw), lambda b, j, i: (0, j))],
        out_specs=[pl.BlockSpec((tt, cw), lambda b, j, i: (b * nt + i, j)),
                   pl.BlockSpec((1, CONV_HIST, cw), lambda b, j, i: (b, 0, j))],
        out_shape=[jax.ShapeDtypeStruct((batch * seq, d), BF16),
                   jax.ShapeDtypeStruct((batch, CONV_HIST, d), F32)],
        scratch_shapes=[pltpu.VMEM((CONV_HIST + tt, cw), F32),
                        pltpu.VMEM((SUBLANES - 1, tt + CONV_HIST - SUBLANES, cw), F32)],
        compiler_params=_cparams("parallel", "parallel", "arbitrary"),
        name=name,
    )(proj, proj, hist, dw_w, dw_b.reshape(1, d))


def _softplus(x):
    return jnp.maximum(x, 0.0) + jnp.log1p(jnp.exp(-jnp.abs(x)))


def _ssd_kernel(z_ref, x_ref, b_ref, c_ref, dt_ref, hx_ref, hb_ref, hc_ref, st0_ref,
                cwx_ref, cwb_ref, cwc_ref, cbx_ref, cbb_ref, cbc_ref,
                dtal_ref, dskip_ref, gn_ref, expb_ref,
                y_ref, tx_ref, tb_ref, tc_ref, st_ref,
                xbuf, bbuf, cbuf, state, *, L, valid):
    ci = pl.program_id(2)

    @pl.when(ci == 0)
    def _():
        xbuf[0:SSM_HIST, :] = hx_ref[0]
        bbuf[0:SSM_HIST, :] = hb_ref[0]
        cbuf[0:SSM_HIST, :] = hc_ref[0]
        state[...] = st0_ref[0].T

    def short_conv(buf, new_ref, w_ref, bias_ref, tail_ref):
        new = new_ref[...]
        buf[SSM_HIST:SSM_HIST + L, :] = new.astype(F32)
        off = SSM_HIST - (SSM_CONV_K - 1)

        def direct(rows):
            acc = buf[off:off + rows, :] * w_ref[0:1, :]
            for k in range(1, SSM_CONV_K):
                acc = acc + buf[off + k:off + k + rows, :] * w_ref[k:k + 1, :]
            return acc

        if L % LANE == 0:
            last = SSM_CONV_K - 1
            cur = new.astype(F32)
            acc = cur * w_ref[last:last + 1, :]
            for j in range(1, SSM_CONV_K):
                acc = acc + pltpu.roll(cur, shift=j, axis=0) * w_ref[last - j:last - j + 1, :]
            acc = jnp.concatenate([direct(SUBLANES), acc[SUBLANES:, :]], axis=0)
        else:
            acc = direct(L)
        tail = buf[valid:valid + SSM_HIST, :]
        tail_ref[0] = tail
        buf[0:SSM_HIST, :] = tail
        return _silu(acc + bias_ref[...])

    x = short_conv(xbuf, x_ref, cwx_ref, cbx_ref, tx_ref)
    bm = short_conv(bbuf, b_ref, cwb_ref, cbb_ref, tb_ref)
    cm = short_conv(cbuf, c_ref, cwc_ref, cbc_ref, tc_ref)

    row = lax.broadcasted_iota(jnp.int32, (L, L), 0)
    col = lax.broadcasted_iota(jnp.int32, (L, L), 1)
    causal = col <= row
    dt = _softplus(dt_ref[...] + dtal_ref[0, 0:1, :])
    if valid < L:
        dt = jnp.where(lax.broadcasted_iota(jnp.int32, dt.shape, 0) < valid, dt, 0.0)
    da = dt * (-jnp.exp(dtal_ref[0, 1:2, :]))
    acs = _dot_onehot_lhs(causal.astype(BF16), da)
    acs_last = acs[L - 1:L, :]
    eacs = jnp.exp(acs)
    dte = jnp.exp(acs_last - acs) * dt
    acs_t = acs.T
    dt_t = dt.T

    eacs_x = _dot(eacs.astype(BF16), expb_ref[...])
    dte_x = _dot(dte.astype(BF16), expb_ref[...])
    cdec_x = _dot_onehot_rhs(jnp.broadcast_to(jnp.exp(acs_last), (SUBLANES, LANE)), expb_ref[...])[0:1, :]

    x_b = x.astype(BF16)
    bm_b = bm.astype(BF16)
    cm_b = cm.astype(BF16)
    cb = _dot_nt(cm_b, bm_b)
    st = state[...]
    y_off = _dot(cm_b, st.astype(BF16)) * eacs_x

    low_half = lax.broadcasted_iota(jnp.int32, (L, LANE), 1) < SSM_HEAD_DIM
    pairs = []
    for pr in range(SSM_GROUP_HEADS // 2):
        xp = x_b[:, pr * LANE:(pr + 1) * LANE]
        ys = []
        for e in (2 * pr, 2 * pr + 1):
            seg = jnp.where(causal, jnp.exp(acs[:, e:e + 1] - acs_t[e:e + 1, :]), 0.0)
            mm = (cb * seg * dt_t[e:e + 1, :]).astype(BF16)
            ys.append(_dot(mm, xp))
        pairs.append(jnp.where(low_half, ys[0], ys[1]))
    y = jnp.concatenate(pairs, axis=1) + y_off + dskip_ref[...] * x

    state[...] = st * cdec_x + _dot(bm.T.astype(BF16), (x * dte_x).astype(BF16))

    yz = y * _silu(z_ref[...].astype(F32))
    yn = (yz * lax.rsqrt(jnp.mean(yz * yz, axis=-1, keepdims=True) + EPS)) * gn_ref[...]
    y_ref[...] = yn.astype(y_ref.dtype)
    st_ref[0] = state[...].T


def ssd_mixer(proj, dtp, hist, st0, conv_w, conv_b, dtal, dskip, gnorm, *, batch, nchunks, L, valid, name):
    gw, ns = SSM_GROUP_W, D_STATE
    xb = (SSM_GROUPS * gw) // gw
    bb = (2 * SSM_GROUPS * gw) // ns
    cb = bb + SSM_GROUPS
    pb = (SSM_GROUPS * gw) // ns
    pc = pb + SSM_GROUPS
    rows = batch * nchunks * L
    rmap = lambda off: (lambda b, g, c: (b * nchunks + c, off + g))
    hmap = lambda off: (lambda b, g, c: (b, 0, off + g))
    pmap = lambda off: (lambda b, g, c: (0, off + g))
    in_specs = [
        pl.BlockSpec((L, gw), rmap(0)), pl.BlockSpec((L, gw), rmap(xb)),
        pl.BlockSpec((L, ns), rmap(bb)), pl.BlockSpec((L, ns), rmap(cb)),
        pl.BlockSpec((L, LANE), rmap(0)),
        pl.BlockSpec((1, SSM_HIST, gw), hmap(0)), pl.BlockSpec((1, SSM_HIST, ns), hmap(pb)),
        pl.BlockSpec((1, SSM_HIST, ns), hmap(pc)),
        pl.BlockSpec((1, gw, ns), lambda b, g, c: (b, g, 0)),
        pl.BlockSpec((SSM_CONV_K, gw), pmap(0)), pl.BlockSpec((SSM_CONV_K, ns), pmap(pb)),
        pl.BlockSpec((SSM_CONV_K, ns), pmap(pc)),
        pl.BlockSpec((1, gw), pmap(0)), pl.BlockSpec((1, ns), pmap(pb)), pl.BlockSpec((1, ns), pmap(pc)),
        pl.BlockSpec((1, 2, LANE), lambda b, g, c: (g, 0, 0)),
        pl.BlockSpec((1, gw), pmap(0)), pl.BlockSpec((1, gw), pmap(0)),
        _resident((LANE, gw), lambda b, g, c: (0, 0)),
    ]
    expand = (np.arange(gw)[None, :] // SSM_HEAD_DIM) == np.arange(LANE)[:, None]
    out_specs = [
        pl.BlockSpec((L, gw), rmap(0)),
        pl.BlockSpec((1, SSM_HIST, gw), hmap(0)), pl.BlockSpec((1, SSM_HIST, ns), hmap(0)),
        pl.BlockSpec((1, SSM_HIST, ns), hmap(0)),
        pl.BlockSpec((1, gw, ns), lambda b, g, c: (b, g, 0)),
    ]
    out_shape = [
        jax.ShapeDtypeStruct((rows, SSM_GROUPS * gw), BF16),
        jax.ShapeDtypeStruct((batch, SSM_HIST, SSM_GROUPS * gw), F32),
        jax.ShapeDtypeStruct((batch, SSM_HIST, SSM_GROUPS * ns), F32),
        jax.ShapeDtypeStruct((batch, SSM_HIST, SSM_GROUPS * ns), F32),
        jax.ShapeDtypeStruct((batch, SSM_GROUPS * gw, ns), F32),
    ]
    return pl.pallas_call(
        functools.partial(_ssd_kernel, L=L, valid=valid),
        grid=(batch, SSM_GROUPS, nchunks),
        in_specs=in_specs, out_specs=out_specs, out_shape=out_shape,
        scratch_shapes=[pltpu.VMEM((SSM_HIST + L, gw), F32), pltpu.VMEM((SSM_HIST + L, ns), F32),
                        pltpu.VMEM((SSM_HIST + L, ns), F32), pltpu.VMEM((ns, gw), F32)],
        compiler_params=_cparams("parallel", "parallel", "arbitrary"),
        name=name,
    )(proj, proj, proj, proj, dtp, hist, hist, hist, st0,
      conv_w, conv_w, conv_w, conv_b, conv_b, conv_b, dtal, dskip, gnorm, jnp.asarray(expand, BF16))


def _rope_tables(pos):
    half = ROPE // 2
    freqs = np.float32(ROPE_THETA) ** (-np.arange(half, dtype=np.float32) / np.float32(half))
    ang = pos.astype(np.float32)[:, None] * freqs[None, :]
    zeros = np.zeros((pos.shape[0], LANE - ROPE), np.float32)
    cos, sin = np.cos(ang), np.sin(ang)
    return np.concatenate([cos, cos, zeros], axis=1), np.concatenate([sin, sin, zeros], axis=1)


def _rot_cols(w):
    half = ROPE // 2
    return jnp.concatenate([-w[..., half:], w[..., :half]], axis=-1)


def _mla_weights(w_in, w_qb, w_kvb, w_out):
    d = w_in.shape[0]
    qa, ckv = w_in[:, :LORA], w_in[:, LORA:2 * LORA]
    kpe, gate = w_in[:, 2 * LORA:2 * LORA + ROPE], w_in[:, 2 * LORA + ROPE:]
    z = jnp.zeros((d, LANE - ROPE), F32)
    w0 = jnp.concatenate([gate, qa, ckv, kpe, z, _rot_cols(kpe), z], axis=1).astype(BF16)
    hw = HEADS * NOPE
    r = w_qb[:, :, NOPE:]
    wq = jnp.concatenate([w_qb[:, :, :NOPE], r, _rot_cols(r)], axis=-1)
    return dict(
        w0=w0,
        wq=wq.reshape(LORA, HEADS * Q_COLS).astype(BF16),
        wk=w_kvb[:, :, :NOPE].reshape(LORA, hw).astype(BF16),
        wvt=w_kvb[:, :, NOPE:].reshape(LORA, hw).T.astype(BF16),
        wo=w_out.astype(BF16),
    )


def _mla_layer(xp, xs, norm_g, q_norm, kv_norm, wts, ckv_cache, kpe_cache, tabs_p, tabs_s,
               *, batch, seq, fin, tag):
    mid = functools.partial(mla_mid, q_norm=q_norm, kv_norm=kv_norm, wq=wts["wq"], wk=wts["wk"], wvt=wts["wvt"])
    proj_p = norm_matmul(xp, norm_g, wts["w0"], tm=512, tn=MLA_PROJ_W, out_dtype=BF16, name=tag + "_inproj_p")
    q, ckv_p, kpe_p, k, vt = mid(proj_p, *tabs_p, tm=VT_COLS, with_kv=True, name=tag + "_mid_p")
    o = flash_attention(q, k, vt, batch=batch, seq=seq, tile=min(seq, 1024), name=tag + "_flash")
    xp = out_proj(o, proj_p, wts["wo"], xp, fin, tm=256, name=tag + "_outproj_p")

    ms = xs.shape[0]
    proj_s = norm_matmul(xs, norm_g, wts["w0"], tm=ms, tn=MLA_PROJ_W, out_dtype=BF16, name=tag + "_inproj_s")
    qs, ckv_s, kpe_s = mid(proj_s, *tabs_s, tm=ms, with_kv=False, name=tag + "_mid_s")
    os_ = sample_attention(qs, ckv_s, kpe_s, ckv_cache, kpe_cache, wts["wk"], wts["wvt"], name=tag + "_attn_s")
    xs = out_proj(os_, proj_s, wts["wo"], xs, fin, tm=ms, name=tag + "_outproj_s")
    return xp, xs, ckv_p, kpe_p, ckv_s, kpe_s


def _ssd_layer(xp, xs, l2_norm, l2_w_in, l2_conv_w, l2_conv_b, l2_dt_bias, l2_a_log, l2_d_skip, l2_gnorm, l2_w_out,
               state_l2_conv, state_l2_ssm, *, batch, seq, nb, new, chunk_p=256, chunk_s=None):
    d = xp.shape[1]
    ms = nb * new
    inner = SSM_GROUPS * SSM_GROUP_W
    bc = SSM_GROUPS * D_STATE
    main_w = 2 * inner + 2 * bc
    w2 = l2_w_in.astype(BF16)
    w2dt = jnp.pad(l2_w_in[:, main_w:].reshape(d, SSM_GROUPS, SSM_GROUP_HEADS),
                   ((0, 0), (0, 0), (0, LANE - SSM_GROUP_HEADS))).reshape(d, SSM_GROUPS * LANE).astype(BF16)
    w2o = l2_w_out.astype(BF16)
    dtal = jnp.pad(jnp.stack([l2_dt_bias, l2_a_log]).reshape(2, SSM_GROUPS, SSM_GROUP_HEADS).transpose(1, 0, 2),
                   ((0, 0), (0, 0), (0, LANE - SSM_GROUP_HEADS)))
    dskip = jnp.repeat(l2_d_skip, SSM_HEAD_DIM).reshape(1, inner)
    hist_rows = SSM_CONV_K - 1

    def run_ssd(proj, dtp, hist, st0, nbatch, nchunks, L, valid, name):
        hist = jnp.pad(hist, ((0, 0), (SSM_HIST - hist_rows, 0), (0, 0)))
        y, tx, tb, tc, st = ssd_mixer(
            proj, dtp, hist, st0.reshape(nbatch, inner, D_STATE), l2_conv_w, l2_conv_b.reshape(1, -1), dtal, dskip,
            l2_gnorm.reshape(1, inner), batch=nbatch, nchunks=nchunks, L=L, valid=valid, name=name)
        tail = jnp.concatenate([tx, tb, tc], axis=-1)[:, SSM_HIST - hist_rows:]
        return y, tail, st.reshape(nbatch, SSM_GROUPS * SSM_GROUP_HEADS, SSM_HEAD_DIM, D_STATE)

    proj_p = norm_matmul(xp, l2_norm, w2, tm=256, tn=main_w // 2, n_out=main_w, out_dtype=BF16, name="l2_inproj_p")
    dt_p = norm_matmul(xp, l2_norm, w2dt, tm=512, tn=SSM_GROUPS * LANE, out_dtype=F32, name="l2_dtproj_p")
    y_p, l2_conv_p, l2_ssm_p = run_ssd(
        proj_p, dt_p, jnp.zeros((batch, hist_rows, inner + 2 * bc), F32),
        jnp.zeros((batch, inner, D_STATE), F32), batch, seq // chunk_p, chunk_p, chunk_p, "l2_ssd_p")
    xp = out_proj(y_p, None, w2o, xp, None, tm=256, name="l2_outproj_p")

    proj_s = norm_matmul(xs, l2_norm, w2, tm=ms, tn=main_w // 2, n_out=main_w, out_dtype=BF16, name="l2_inproj_s")
    dt_s = norm_matmul(xs, l2_norm, w2dt, tm=ms, tn=SSM_GROUPS * LANE, out_dtype=F32, name="l2_dtproj_s")
    L = chunk_s if chunk_s is not None else -(-new // SUBLANES) * SUBLANES
    pad_rows = lambda a: jnp.pad(a.reshape(nb, new, a.shape[1]), ((0, 0), (0, L - new), (0, 0))).reshape(nb * L, a.shape[1])
    y_s, l2_conv_s, l2_ssm_s = run_ssd(pad_rows(proj_s), pad_rows(dt_s), state_l2_conv, state_l2_ssm,
                                       nb, 1, L, new, "l2_ssd_s")
    y_s = y_s.reshape(nb, L, inner)[:, :new].reshape(ms, inner)
    xs = out_proj(y_s, None, w2o, xs, None, tm=ms, name="l2_outproj_s")

    return xp, xs, l2_conv_p, l2_ssm_p, l2_conv_s, l2_ssm_s


def kernel(x_prompt, x_sample, cache_l0_ckv, cache_l0_kpe, state_l1_conv, state_l2_conv, state_l2_ssm, cache_l3_ckv, cache_l3_kpe, l0_norm, l0_w_in, l0_q_norm, l0_w_qb, l0_kv_norm, l0_w_kvb, l0_w_out, l1_norm, l1_w_in, l1_dw_w, l1_dw_b, l1_ln_g, l1_ln_b, l1_w_out, l2_norm, l2_w_in, l2_conv_w, l2_conv_b, l2_dt_bias, l2_a_log, l2_d_skip, l2_gnorm, l2_w_out, l3_norm, l3_w_in, l3_q_norm, l3_w_qb, l3_kv_norm, l3_w_kvb, l3_w_out, final_norm):
    batch, seq, d = x_prompt.shape
    nb, new, _ = x_sample.shape
    past = cache_l0_ckv.shape[1]
    xp = x_prompt.reshape(batch * seq, d)
    xs = x_sample.reshape(nb * new, d)
    ms = nb * new

    tabs_p = _rope_tables(np.arange(seq))
    cs, sn = _rope_tables(past + np.arange(new))
    tabs_s = (np.tile(cs, (nb, 1)), np.tile(sn, (nb, 1)))

    w0 = _mla_weights(l0_w_in, l0_w_qb, l0_w_kvb, l0_w_out)
    xp, xs, l0_ckv_p, l0_kpe_p, l0_ckv_s, l0_kpe_s = _mla_layer(
        xp, xs, l0_norm, l0_q_norm, l0_kv_norm, w0, cache_l0_ckv, cache_l0_kpe, tabs_p, tabs_s,
        batch=batch, seq=seq, fin=None, tag="l0")

    w1 = l1_w_in.astype(BF16)
    w1o = l1_w_out.astype(BF16)
    pad_hist = lambda h: jnp.pad(h, ((0, 0), (CONV_HIST - (CONV_K - 1), 0), (0, 0)))
    proj_p = norm_matmul(xp, l1_norm, w1, tm=512, tn=w1.shape[1] // 2, out_dtype=BF16, name="l1_inproj_p")
    ln1 = (l1_ln_g, l1_ln_b)
    gate_block = 2
    c_p, tail_p = conv_mixer(proj_p, jnp.zeros((batch, CONV_HIST, d), F32), l1_dw_w, l1_dw_b,
                             batch=batch, seq=seq, tt=256, name="l1_conv_p")
    xp = out_proj(c_p, proj_p, w1o, xp, None, tm=256, gate_block=gate_block, ln=ln1, name="l1_outproj_p")
    proj_s = norm_matmul(xs, l1_norm, w1, tm=ms, tn=w1.shape[1] // 2, out_dtype=BF16, name="l1_inproj_s")
    c_s, tail_s = conv_mixer(proj_s, pad_hist(state_l1_conv), l1_dw_w, l1_dw_b,
                             batch=nb, seq=new, tt=new, name="l1_conv_s")
    xs = out_proj(c_s, proj_s, w1o, xs, None, tm=ms, gate_block=gate_block, ln=ln1, name="l1_outproj_s")
    l1_conv_p = tail_p[:, CONV_HIST - (CONV_K - 1):]
    l1_conv_s = tail_s[:, CONV_HIST - (CONV_K - 1):]

    xp, xs, l2_conv_p, l2_ssm_p, l2_conv_s, l2_ssm_s = _ssd_layer(
        xp, xs, l2_norm, l2_w_in, l2_conv_w, l2_conv_b, l2_dt_bias, l2_a_log, l2_d_skip, l2_gnorm, l2_w_out,
        state_l2_conv, state_l2_ssm, batch=batch, seq=seq, nb=nb, new=new)

    w3 = _mla_weights(l3_w_in, l3_w_qb, l3_w_kvb, l3_w_out)
    yp, ys, l3_ckv_p, l3_kpe_p, l3_ckv_s, l3_kpe_s = _mla_layer(
        xp, xs, l3_norm, l3_q_norm, l3_kv_norm, w3, cache_l3_ckv, cache_l3_kpe, tabs_p, tabs_s,
        batch=batch, seq=seq, fin=final_norm, tag="l3")

    r3 = lambda a, n: a.reshape(n, -1, a.shape[-1])
    return (yp.reshape(batch, seq, d), ys.reshape(nb, new, d),
            r3(l0_ckv_p, batch), r3(l0_kpe_p, batch), r3(l0_ckv_s, nb), r3(l0_kpe_s, nb),
            l1_conv_p, l1_conv_s,
            l2_conv_p, l2_ssm_p, l2_conv_s, l2_ssm_s,
            r3(l3_ckv_p, batch), r3(l3_kpe_p, batch), r3(l3_ckv_s, nb), r3(l3_kpe_s, nb))
```

```python
import functools

import jax
import jax.numpy as jnp
import numpy as np
from jax import lax
from jax.experimental import pallas as pl
from jax.experimental.pallas import tpu as pltpu

F32 = jnp.float32
BF16 = jnp.bfloat16

EPS = 1e-6
ROPE_THETA = 10000.0
CHUNK = 64
HEADS = 16
NOPE = 128
ROPE = 64
HEAD_PAD = 256
MLA_SCALE = (NOPE + ROPE) ** -0.5
LOG2E = 1.4426950408889634
LORA = 512
CONV_K = 31
CONV_HIST = 32
SSM_GROUPS = 8
SSM_GROUP_HEADS = 8
SSM_HEAD_DIM = 64
SSM_GROUP_W = SSM_GROUP_HEADS * SSM_HEAD_DIM
D_STATE = 128
SSM_CONV_K = 4
SSM_HIST = 8
LANE = 128

VMEM_LIMIT_BYTES = 56 * 1024 * 1024


def _cparams(*sem):
    return pltpu.CompilerParams(dimension_semantics=sem, vmem_limit_bytes=VMEM_LIMIT_BYTES)


def _resident(shape, index_map):
    return pl.BlockSpec(shape, index_map, pipeline_mode=pl.Buffered(1))


def _sigmoid(x):
    return 0.5 * jnp.tanh(0.5 * x) + 0.5


def _silu(x):
    h = 0.5 * x
    return h + h * jnp.tanh(h)


def _split3(x):
    hi = x.astype(BF16)
    r = x - hi.astype(F32)
    mid = r.astype(BF16)
    return hi, mid, (r - mid.astype(F32)).astype(BF16)


def _dot_onehot_lhs(sel, x):
    hi, mid, lo = _split3(x)
    return _dot(sel, hi) + (_dot(sel, mid) + _dot(sel, lo))


def _dot_onehot_rhs(x, sel):
    hi, mid, lo = _split3(x)
    return _dot(hi, sel) + (_dot(mid, sel) + _dot(lo, sel))


def _rms(x, g):
    return (x * lax.rsqrt(jnp.mean(x * x, axis=-1, keepdims=True) + EPS)) * g


def _dot(a, b):
    return jnp.dot(a, b, preferred_element_type=F32)


def _dot_nt(a, b):
    return lax.dot_general(a, b, (((1,), (1,)), ((), ())), preferred_element_type=F32)


def _norm_matmul_kernel(x_ref, g_ref, w_ref, o_ref, *, col_chunk):
    xn = _rms(x_ref[...], g_ref[...]).astype(BF16)
    for c in range(0, o_ref.shape[1], col_chunk):
        o_ref[:, c:c + col_chunk] = _dot(xn, w_ref[:, c:c + col_chunk]).astype(o_ref.dtype)


def norm_matmul(x, g, w, *, tm, tn, out_dtype, name, n_out=None):
    m, k = x.shape
    n = w.shape[1] if n_out is None else n_out
    col_chunk = 256 if tn % 256 == 0 else LANE
    return pl.pallas_call(
        functools.partial(_norm_matmul_kernel, col_chunk=col_chunk),
        grid=(n // tn, m // tm),
        in_specs=[pl.BlockSpec((tm, k), lambda j, i: (i, 0)),
                  _resident((1, k), lambda j, i: (0, 0)),
                  _resident((k, tn), lambda j, i: (0, j))],
        out_specs=pl.BlockSpec((tm, tn), lambda j, i: (i, j)),
        out_shape=jax.ShapeDtypeStruct((m, n), out_dtype),
        compiler_params=_cparams("arbitrary", "arbitrary"),
        name=name,
    )(x, g.reshape(1, k), w)


def _out_proj_kernel(*refs, gated, layer_norm, final):
    refs = list(refs)
    a_ref = refs.pop(0)
    gate_ref = refs.pop(0) if gated else None
    lng_ref, lnb_ref = (refs.pop(0), refs.pop(0)) if layer_norm else (None, None)
    w_ref, res_ref = refs.pop(0), refs.pop(0)
    fin_ref = refs.pop(0) if final else None
    o_ref = refs.pop(0)
    a = a_ref[...]
    if layer_norm:
        a = a.astype(F32)
        cen = a - jnp.mean(a, axis=-1, keepdims=True)
        var = jnp.mean(cen * cen, axis=-1, keepdims=True)
        a = _silu((cen * lax.rsqrt(var + EPS)) * lng_ref[...] + lnb_ref[...])
    if gated:
        a = a.astype(F32) * _silu(gate_ref[...].astype(F32))
    y = res_ref[...] + _dot(a.astype(BF16), w_ref[...])
    if final:
        y = _rms(y, fin_ref[...])
    o_ref[...] = y


def out_proj(a, gate_src, w, res, fin, *, tm, name, gate_block=0, ln=None):
    m, k = a.shape
    n = w.shape[1]
    gated, final, layer_norm = gate_src is not None, fin is not None, ln is not None
    in_specs = [pl.BlockSpec((tm, k), lambda i: (i, 0))]
    args = [a]
    if gated:
        in_specs.append(pl.BlockSpec((tm, k), lambda i: (i, gate_block)))
        args.append(gate_src)
    if layer_norm:
        in_specs += [_resident((1, k), lambda i: (0, 0)), _resident((1, k), lambda i: (0, 0))]
        args += [ln[0].reshape(1, k), ln[1].reshape(1, k)]
    in_specs += [_resident((k, n), lambda i: (0, 0)), pl.BlockSpec((tm, n), lambda i: (i, 0))]
    args += [w, res]
    if final:
        in_specs.append(_resident((1, n), lambda i: (0, 0)))
        args.append(fin.reshape(1, n))
    return pl.pallas_call(
        functools.partial(_out_proj_kernel, gated=gated, layer_norm=layer_norm, final=final),
        grid=(m // tm,),
        in_specs=in_specs,
        out_specs=pl.BlockSpec((tm, n), lambda i: (i, 0)),
        out_shape=jax.ShapeDtypeStruct((m, n), F32),
        compiler_params=_cparams("parallel"),
        name=name,
    )(*args)


MLA_GATE_W = HEADS * NOPE
MLA_PROJ_W = MLA_GATE_W + 2 * LORA + 2 * LANE


def _mla_mid_kernel(*refs, with_kv):
    qa_ref, ckv_ref, kpe_ref, cos_ref, sin_ref, qn_ref, kvn_ref, wq_ref = refs[:8]
    refs = refs[8:]
    if with_kv:
        wk_ref, wvt_ref, q_out, ckv_out, kpe_out, k_out, v_out = refs
    else:
        q_out, ckv_out, kpe_out = refs
    cosp, sinp = cos_ref[...], sin_ref[...]
    q_scale = MLA_SCALE * LOG2E if with_kv else MLA_SCALE

    qa = _rms(qa_ref[...].astype(F32), qn_ref[...]).astype(BF16)
    qall = _dot(qa, wq_ref[...])
    for h in range(HEADS):
        lo, out = h * Q_COLS, h * HEAD_PAD
        q_out[:, out:out + LANE] = (qall[:, lo:lo + LANE] * q_scale).astype(BF16)
        pair = qall[:, lo + LANE:lo + 2 * LANE]
        rot = (pair * cosp + pltpu.roll(pair, ROPE, axis=1) * sinp) * q_scale
        q_out[:, out + LANE:out + 2 * LANE] = rot.astype(BF16)

    ckv = _rms(ckv_ref[...].astype(F32), kvn_ref[...])
    ckv_out[...] = ckv
    kp = kpe_ref[...].astype(F32)
    kpe = kp[:, :LANE] * cosp + kp[:, LANE:] * sinp
    kpe_out[...] = kpe[:, :ROPE]
    if with_kv:
        ckv_b = ckv.astype(BF16)
        kpe_b = kpe.astype(BF16)
        kn = _dot(ckv_b, wk_ref[...])
        for h in range(HEADS):
            lo = h * LANE
            k_out[:, 2 * lo:2 * lo + LANE] = kn[:, lo:lo + LANE].astype(BF16)
            k_out[:, 2 * lo + LANE:2 * lo + 2 * LANE] = kpe_b
        v_out[0] = _dot_nt(wvt_ref[...], ckv_b).astype(BF16)


Q_COLS = 2 * LANE


def mla_mid(proj, cos_t, sin_t, q_norm, kv_norm, wq, wk, wvt, *, tm, with_kv, name):
    m = proj.shape[0]
    nt = cos_t.shape[0] // tm
    hw = HEADS * LANE
    in_specs = [
        pl.BlockSpec((tm, LORA), lambda i: (i, MLA_GATE_W // LORA)),
        pl.BlockSpec((tm, LORA), lambda i: (i, MLA_GATE_W // LORA + 1)),
        pl.BlockSpec((tm, 2 * LANE), lambda i: (i, (MLA_GATE_W + 2 * LORA) // (2 * LANE))),
        pl.BlockSpec((tm, LANE), lambda i: (i % nt, 0)),
        pl.BlockSpec((tm, LANE), lambda i: (i % nt, 0)),
        _resident((1, LORA), lambda i: (0, 0)),
        _resident((1, LORA), lambda i: (0, 0)),
        _resident((LORA, HEADS * Q_COLS), lambda i: (0, 0)),
    ]
    args = [proj, proj, proj, cos_t, sin_t, q_norm.reshape(1, LORA), kv_norm.reshape(1, LORA), wq]
    out_specs = [pl.BlockSpec((tm, HEADS * HEAD_PAD), lambda i: (i, 0)),
                 pl.BlockSpec((tm, LORA), lambda i: (i, 0)),
                 pl.BlockSpec((tm, ROPE), lambda i: (i, 0))]
    out_shape = [jax.ShapeDtypeStruct((m, HEADS * HEAD_PAD), BF16),
                 jax.ShapeDtypeStruct((m, LORA), F32),
                 jax.ShapeDtypeStruct((m, ROPE), F32)]
    if with_kv:
        assert tm == VT_COLS
        in_specs += [_resident((LORA, hw), lambda i: (0, 0)), _resident((hw, LORA), lambda i: (0, 0))]
        args += [wk, wvt]
        out_specs += [pl.BlockSpec((tm, HEADS * HEAD_PAD), lambda i: (i, 0)),
                      pl.BlockSpec((1, hw, tm), lambda i: (i, 0, 0))]
        out_shape += [jax.ShapeDtypeStruct((m, HEADS * HEAD_PAD), BF16),
                      jax.ShapeDtypeStruct((m // tm, hw, tm), BF16)]
    return pl.pallas_call(
        functools.partial(_mla_mid_kernel, with_kv=with_kv),
        grid=(m // tm,),
        in_specs=in_specs, out_specs=out_specs, out_shape=out_shape,
        compiler_params=_cparams("parallel"),
        name=name,
    )(*args)


NEG_BIG = -1e30


FLASH_HEADS = 2


VT_COLS = 256


def _flash_kernel(q_ref, k_ref, vt_ref, o_ref, *, tile):
    qi = pl.program_id(2)
    slabs = tile // VT_COLS

    def step(ki, carry, masked):
        start = pl.multiple_of(ki * tile, tile)
        out = []
        for h in range(FLASH_HEADS):
            m_prev, l_prev, acc = carry[h]
            q = q_ref[:, h * HEAD_PAD:(h + 1) * HEAD_PAD]
            st = _dot_nt(k_ref[pl.ds(start, tile), h * HEAD_PAD:(h + 1) * HEAD_PAD], q)
            if masked:
                kc = lax.broadcasted_iota(jnp.int32, st.shape, 0) // CHUNK
                qc = lax.broadcasted_iota(jnp.int32, st.shape, 1) // CHUNK
                st = jnp.where(kc <= qc, st, NEG_BIG)
            m_new = jnp.maximum(m_prev, jnp.max(st, axis=0, keepdims=True))
            alpha = jnp.exp2(m_prev - m_new)
            p = jnp.exp2(st - m_new)
            l_new = alpha * l_prev + jnp.sum(p, axis=0, keepdims=True)
            p = p.astype(BF16)
            pv = None
            for j in range(slabs):
                vt = vt_ref[ki * slabs + j, h * NOPE:(h + 1) * NOPE, :]
                part = _dot(vt, p[j * VT_COLS:(j + 1) * VT_COLS, :])
                pv = part if pv is None else pv + part
            out.append((m_new, l_new, alpha * acc + pv))
        return tuple(out)

    init = tuple((jnp.full((1, tile), NEG_BIG, F32), jnp.zeros((1, tile), F32), jnp.zeros((NOPE, tile), F32))
                 for _ in range(FLASH_HEADS))
    carry = lax.fori_loop(0, qi, lambda ki, c: step(ki, c, False), init)
    carry = step(qi, carry, True)
    for h in range(FLASH_HEADS):
        _, l_fin, acc = carry[h]
        o_ref[:, h * NOPE:(h + 1) * NOPE] = (acc / l_fin).T.astype(o_ref.dtype)


def flash_attention(q, k, vt, *, batch, seq, tile, name):
    nq = seq // tile
    hq, hv = FLASH_HEADS * HEAD_PAD, FLASH_HEADS * NOPE
    return pl.pallas_call(
        functools.partial(_flash_kernel, tile=tile),
        grid=(batch, HEADS // FLASH_HEADS, nq),
        in_specs=[pl.BlockSpec((tile, hq), lambda b, h, i: (b * nq + i, h)),
                  pl.BlockSpec((seq, hq), lambda b, h, i: (b, h)),
                  pl.BlockSpec((seq // VT_COLS, hv, VT_COLS), lambda b, h, i: (b, h, 0))],
        out_specs=pl.BlockSpec((tile, hv), lambda b, h, i: (b * nq + i, h)),
        out_shape=jax.ShapeDtypeStruct((batch * seq, HEADS * NOPE), BF16),
        compiler_params=_cparams("parallel", "parallel", "arbitrary"),
        name=name,
    )(q, k, vt)


def _sample_attn_kernel(q_ref, cnew_ref, pnew_ref, ccache_ref, pcache_ref, wk_ref, wvt_ref, o_ref,
                        kc_scr, kp_scr, *, past, new, pad):
    total = past + pad
    kc_scr[0:past, :] = ccache_ref[0].astype(BF16)
    kp_scr[0:past, :] = pcache_ref[0].astype(BF16)
    kc_scr[past:total, :] = jnp.zeros((pad, LORA), BF16)
    kp_scr[past:total, :] = jnp.zeros((pad, ROPE), BF16)
    kc_scr[past:past + new, :] = cnew_ref[...].astype(BF16)
    kp_scr[past:past + new, :] = pnew_ref[...].astype(BF16)

    q = q_ref[...]
    qlat, qpe = [], []
    for h in range(HEADS):
        lo = h * HEAD_PAD
        qlat.append(_dot_nt(q[:, lo:lo + NOPE], wk_ref[:, h * NOPE:(h + 1) * NOPE]))
        qpe.append(q[:, lo + NOPE:lo + NOPE + ROPE])
    qlat = jnp.concatenate(qlat, axis=0).astype(BF16)
    qpe = jnp.concatenate(qpe, axis=0)
    kc = kc_scr[...]
    s = _dot_nt(qlat, kc) + _dot_nt(qpe, kp_scr[...])
    col = lax.broadcasted_iota(jnp.int32, s.shape, 1)
    s = jnp.where(col < past + new, s, NEG_BIG)
    m = jnp.max(s, axis=-1, keepdims=True)
    p = jnp.exp(s - m)
    l = jnp.sum(p, axis=-1, keepdims=True)
    olat = (_dot(p.astype(BF16), kc) / l).astype(BF16)
    for h in range(HEADS):
        o_ref[:, h * NOPE:(h + 1) * NOPE] = _dot_nt(
            olat[h * new:(h + 1) * new, :], wvt_ref[h * NOPE:(h + 1) * NOPE, :]).astype(o_ref.dtype)


def sample_attention(q, ckv_new, kpe_new, ckv_cache, kpe_cache, wk, wvt, *, name):
    nb, past, _ = ckv_cache.shape
    new = q.shape[0] // nb
    pad = LANE
    hw = HEADS * NOPE
    return pl.pallas_call(
        functools.partial(_sample_attn_kernel, past=past, new=new, pad=pad),
        grid=(nb,),
        in_specs=[pl.BlockSpec((new, HEADS * HEAD_PAD), lambda b: (b, 0)),
                  pl.BlockSpec((new, LORA), lambda b: (b, 0)),
                  pl.BlockSpec((new, ROPE), lambda b: (b, 0)),
                  pl.BlockSpec((1, past, LORA), lambda b: (b, 0, 0)),
                  pl.BlockSpec((1, past, ROPE), lambda b: (b, 0, 0)),
                  _resident((LORA, hw), lambda b: (0, 0)),
                  _resident((hw, LORA), lambda b: (0, 0))],
        out_specs=pl.BlockSpec((new, hw), lambda b: (b, 0)),
        out_shape=jax.ShapeDtypeStruct((nb * new, hw), BF16),
        scratch_shapes=[pltpu.VMEM((past + pad, LORA), BF16), pltpu.VMEM((past + pad, ROPE), BF16)],
        compiler_params=_cparams("arbitrary"),
        name=name,
    )(q, ckv_new, kpe_new, ckv_cache, kpe_cache, wk, wvt)


SUBLANES = 8
CONV_COLS = 2048


def _conv_kernel(val_ref, glu_ref, hist_ref, dw_ref, dwb_ref, y_ref, tail_ref, ubuf, shifted, *, tt):
    i = pl.program_id(2)

    @pl.when(i == 0)
    def _():
        ubuf[0:CONV_HIST, :] = hist_ref[0]

    ubuf[CONV_HIST:CONV_HIST + tt, :] = val_ref[...].astype(F32) * _sigmoid(glu_ref[...].astype(F32))
    span = tt + CONV_HIST - SUBLANES
    for r in range(1, SUBLANES):
        shifted[r - 1] = ubuf[r:r + span, :]
    first = CONV_HIST - (CONV_K - 1)
    for c in range(0, ubuf.shape[1], LANE):
        acc = None
        for k in range(CONV_K):
            q, r = divmod(first + k, SUBLANES)
            rows = pl.ds(q * SUBLANES, tt)
            src = ubuf[rows, c:c + LANE] if r == 0 else shifted[r - 1, rows, c:c + LANE]
            term = src * dw_ref[k:k + 1, c:c + LANE]
            acc = term if acc is None else acc + term
        y_ref[:, c:c + LANE] = (acc + dwb_ref[:, c:c + LANE]).astype(y_ref.dtype)
    tail = ubuf[tt:tt + CONV_HIST, :]
    tail_ref[0] = tail
    ubuf[0:CONV_HIST, :] = tail


def conv_mixer(proj, hist, dw_w, dw_b, *, batch, seq, tt, name):
    d = dw_w.shape[1]
    nt, cw = seq // tt, CONV_COLS
    nc = d // cw
    return pl.pallas_call(
        functools.partial(_conv_kernel, tt=tt),
        grid=(batch, nc, nt),
        in_specs=[pl.BlockSpec((tt, cw), lambda b, j, i: (b * nt + i, j)),
                  pl.BlockSpec((tt, cw), lambda b, j, i: (b * nt + i, nc + j)),
                  pl.BlockSpec((1, CONV_HIST, cw), lambda b, j, i: (b, 0, j)),
                  pl.BlockSpec((CONV_K, cw), lambda b, j, i: (0, j)),
                  pl.BlockSpec((1, cw), lambda b, j, i: (0, j))],
        out_specs=[pl.BlockSpec((tt, cw), lambda b, j, i: (b * nt + i, j)),
                   pl.BlockSpec((1, CONV_HIST, cw), lambda b, j, i: (b, 0, j))],
        out_shape=[jax.ShapeDtypeStruct((batch * seq, d), BF16),
                   jax.ShapeDtypeStruct((batch, CONV_HIST, d), F32)],
        scratch_shapes=[pltpu.VMEM((CONV_HIST + tt, cw), F32),
                        pltpu.VMEM((SUBLANES - 1, tt + CONV_HIST - SUBLANES, cw), F32)],
        compiler_params=_cparams("parallel", "parallel", "arbitrary"),
        name=name,
    )(proj, proj, hist, dw_w, dw_b.reshape(1, d))


def _softplus(x):
    return jnp.maximum(x, 0.0) + jnp.log1p(jnp.exp(-jnp.abs(x)))


def _ssd_kernel(z_ref, x_ref, b_ref, c_ref, dt_ref, hx_ref, hb_ref, hc_ref, st0_ref,
                cwx_ref, cwb_ref, cwc_ref, cbx_ref, cbb_ref, cbc_ref,
                dtal_ref, dskip_ref, gn_ref, expb_ref,
                y_ref, tx_ref, tb_ref, tc_ref, st_ref,
                xbuf, bbuf, cbuf, state, *, L, valid):
    ci = pl.program_id(2)

    @pl.when(ci == 0)
    def _():
        xbuf[0:SSM_HIST, :] = hx_ref[0]
        bbuf[0:SSM_HIST, :] = hb_ref[0]
        cbuf[0:SSM_HIST, :] = hc_ref[0]
        state[...] = st0_ref[0].T

    def short_conv(buf, new_ref, w_ref, bias_ref, tail_ref):
        new = new_ref[...]
        buf[SSM_HIST:SSM_HIST + L, :] = new.astype(F32)
        off = SSM_HIST - (SSM_CONV_K - 1)

        def direct(rows):
            acc = buf[off:off + rows, :] * w_ref[0:1, :]
            for k in range(1, SSM_CONV_K):
                acc = acc + buf[off + k:off + k + rows, :] * w_ref[k:k + 1, :]
            return acc

        if L % LANE == 0:
            last = SSM_CONV_K - 1
            cur = new.astype(F32)
            acc = cur * w_ref[last:last + 1, :]
            for j in range(1, SSM_CONV_K):
                acc = acc + pltpu.roll(cur, shift=j, axis=0) * w_ref[last - j:last - j + 1, :]
            acc = jnp.concatenate([direct(SUBLANES), acc[SUBLANES:, :]], axis=0)
        else:
            acc = direct(L)
        tail = buf[valid:valid + SSM_HIST, :]
        tail_ref[0] = tail
        buf[0:SSM_HIST, :] = tail
        return _silu(acc + bias_ref[...])

    x = short_conv(xbuf, x_ref, cwx_ref, cbx_ref, tx_ref)
    bm = short_conv(bbuf, b_ref, cwb_ref, cbb_ref, tb_ref)
    cm = short_conv(cbuf, c_ref, cwc_ref, cbc_ref, tc_ref)

    row = lax.broadcasted_iota(jnp.int32, (L, L), 0)
    col = lax.broadcasted_iota(jnp.int32, (L, L), 1)
    causal = col <= row
    dt = _softplus(dt_ref[...] + dtal_ref[0, 0:1, :])
    if valid < L:
        dt = jnp.where(lax.broadcasted_iota(jnp.int32, dt.shape, 0) < valid, dt, 0.0)
    da = dt * (-jnp.exp(dtal_ref[0, 1:2, :]))
    acs = _dot_onehot_lhs(causal.astype(BF16), da)
    acs_last = acs[L - 1:L, :]
    eacs = jnp.exp(acs)
    dte = jnp.exp(acs_last - acs) * dt
    acs_t = acs.T
    dt_t = dt.T

    eacs_x = _dot(eacs.astype(BF16), expb_ref[...])
    dte_x = _dot(dte.astype(BF16), expb_ref[...])
    cdec_x = _dot_onehot_rhs(jnp.broadcast_to(jnp.exp(acs_last), (SUBLANES, LANE)), expb_ref[...])[0:1, :]

    x_b = x.astype(BF16)
    bm_b = bm.astype(BF16)
    cm_b = cm.astype(BF16)
    cb = _dot_nt(cm_b, bm_b)
    st = state[...]
    y_off = _dot(cm_b, st.astype(BF16)) * eacs_x

    low_half = lax.broadcasted_iota(jnp.int32, (L, LANE), 1) < SSM_HEAD_DIM
    pairs = []
    for pr in range(SSM_GROUP_HEADS // 2):
        xp = x_b[:, pr * LANE:(pr + 1) * LANE]
        ys = []
        for e in (2 * pr, 2 * pr + 1):
            seg = jnp.where(causal, jnp.exp(acs[:, e:e + 1] - acs_t[e:e + 1, :]), 0.0)
            mm = (cb * seg * dt_t[e:e + 1, :]).astype(BF16)
            ys.append(_dot(mm, xp))
        pairs.append(jnp.where(low_half, ys[0], ys[1]))
    y = jnp.concatenate(pairs, axis=1) + y_off + dskip_ref[...] * x

    state[...] = st * cdec_x + _dot(bm.T.astype(BF16), (x * dte_x).astype(BF16))

    yz = y * _silu(z_ref[...].astype(F32))
    yn = (yz * lax.rsqrt(jnp.mean(yz * yz, axis=-1, keepdims=True) + EPS)) * gn_ref[...]
    y_ref[...] = yn.astype(y_ref.dtype)
    st_ref[0] = state[...].T


def ssd_mixer(proj, dtp, hist, st0, conv_w, conv_b, dtal, dskip, gnorm, *, batch, nchunks, L, valid, name):
    gw, ns = SSM_GROUP_W, D_STATE
    xb = (SSM_GROUPS * gw) // gw
    bb = (2 * SSM_GROUPS * gw) // ns
    cb = bb + SSM_GROUPS
    pb = (SSM_GROUPS * gw) // ns
    pc = pb + SSM_GROUPS
    rows = batch * nchunks * L
    rmap = lambda off: (lambda b, g, c: (b * nchunks + c, off + g))
    hmap = lambda off: (lambda b, g, c: (b, 0, off + g))
    pmap = lambda off: (lambda b, g, c: (0, off + g))
    in_specs = [
        pl.BlockSpec((L, gw), rmap(0)), pl.BlockSpec((L, gw), rmap(xb)),
        pl.BlockSpec((L, ns), rmap(bb)), pl.BlockSpec((L, ns), rmap(cb)),
        pl.BlockSpec((L, LANE), rmap(0)),
        pl.BlockSpec((1, SSM_HIST, gw), hmap(0)), pl.BlockSpec((1, SSM_HIST, ns), hmap(pb)),
        pl.BlockSpec((1, SSM_HIST, ns), hmap(pc)),
        pl.BlockSpec((1, gw, ns), lambda b, g, c: (b, g, 0)),
        pl.BlockSpec((SSM_CONV_K, gw), pmap(0)), pl.BlockSpec((SSM_CONV_K, ns), pmap(pb)),
        pl.BlockSpec((SSM_CONV_K, ns), pmap(pc)),
        pl.BlockSpec((1, gw), pmap(0)), pl.BlockSpec((1, ns), pmap(pb)), pl.BlockSpec((1, ns), pmap(pc)),
        pl.BlockSpec((1, 2, LANE), lambda b, g, c: (g, 0, 0)),
        pl.BlockSpec((1, gw), pmap(0)), pl.BlockSpec((1, gw), pmap(0)),
        _resident((LANE, gw), lambda b, g, c: (0, 0)),
    ]
    expand = (np.arange(gw)[None, :] // SSM_HEAD_DIM) == np.arange(LANE)[:, None]
    out_specs = [
        pl.BlockSpec((L, gw), rmap(0)),
        pl.BlockSpec((1, SSM_HIST, gw), hmap(0)), pl.BlockSpec((1, SSM_HIST, ns), hmap(0)),
        pl.BlockSpec((1, SSM_HIST, ns), hmap(0)),
        pl.BlockSpec((1, gw, ns), lambda b, g, c: (b, g, 0)),
    ]
    out_shape = [
        jax.ShapeDtypeStruct((rows, SSM_GROUPS * gw), BF16),
        jax.ShapeDtypeStruct((batch, SSM_HIST, SSM_GROUPS * gw), F32),
        jax.ShapeDtypeStruct((batch, SSM_HIST, SSM_GROUPS * ns), F32),
        jax.ShapeDtypeStruct((batch, SSM_HIST, SSM_GROUPS * ns), F32),
        jax.ShapeDtypeStruct((batch, SSM_GROUPS * gw, ns), F32),
    ]
    return pl.pallas_call(
        functools.partial(_ssd_kernel, L=L, valid=valid),
        grid=(batch, SSM_GROUPS, nchunks),
        in_specs=in_specs, out_specs=out_specs, out_shape=out_shape,
        scratch_shapes=[pltpu.VMEM((SSM_HIST + L, gw), F32), pltpu.VMEM((SSM_HIST + L, ns), F32),
                        pltpu.VMEM((SSM_HIST + L, ns), F32), pltpu.VMEM((ns, gw), F32)],
        compiler_params=_cparams("parallel", "parallel", "arbitrary"),
        name=name,
    )(proj, proj, proj, proj, dtp, hist, hist, hist, st0,
      conv_w, conv_w, conv_w, conv_b, conv_b, conv_b, dtal, dskip, gnorm, jnp.asarray(expand, BF16))


def _rope_tables(pos):
    half = ROPE // 2
    freqs = np.float32(ROPE_THETA) ** (-np.arange(half, dtype=np.float32) / np.float32(half))
    ang = pos.astype(np.float32)[:, None] * freqs[None, :]
    zeros = np.zeros((pos.shape[0], LANE - ROPE), np.float32)
    cos, sin = np.cos(ang), np.sin(ang)
    return np.concatenate([cos, cos, zeros], axis=1), np.concatenate([sin, sin, zeros], axis=1)


def _rot_cols(w):
    half = ROPE // 2
    return jnp.concatenate([-w[..., half:], w[..., :half]], axis=-1)


def _mla_weights(w_in, w_qb, w_kvb, w_out):
    d = w_in.shape[0]
    qa, ckv = w_in[:, :LORA], w_in[:, LORA:2 * LORA]
    kpe, gate = w_in[:, 2 * LORA:2 * LORA + ROPE], w_in[:, 2 * LORA + ROPE:]
    z = jnp.zeros((d, LANE - ROPE), F32)
    w0 = jnp.concatenate([gate, qa, ckv, kpe, z, _rot_cols(kpe), z], axis=1).astype(BF16)
    hw = HEADS * NOPE
    r = w_qb[:, :, NOPE:]
    wq = jnp.concatenate([w_qb[:, :, :NOPE], r, _rot_cols(r)], axis=-1)
    return dict(
        w0=w0,
        wq=wq.reshape(LORA, HEADS * Q_COLS).astype(BF16),
        wk=w_kvb[:, :, :NOPE].reshape(LORA, hw).astype(BF16),
        wvt=w_kvb[:, :, NOPE:].reshape(LORA, hw).T.astype(BF16),
        wo=w_out.astype(BF16),
    )


def _mla_layer(xp, xs, norm_g, q_norm, kv_norm, wts, ckv_cache, kpe_cache, tabs_p, tabs_s,
               *, batch, seq, fin, tag):
    mid = functools.partial(mla_mid, q_norm=q_norm, kv_norm=kv_norm, wq=wts["wq"], wk=wts["wk"], wvt=wts["wvt"])
    proj_p = norm_matmul(xp, norm_g, wts["w0"], tm=512, tn=MLA_PROJ_W, out_dtype=BF16, name=tag + "_inproj_p")
    q, ckv_p, kpe_p, k, vt = mid(proj_p, *tabs_p, tm=VT_COLS, with_kv=True, name=tag + "_mid_p")
    o = flash_attention(q, k, vt, batch=batch, seq=seq, tile=min(seq, 1024), name=tag + "_flash")
    xp = out_proj(o, proj_p, wts["wo"], xp, fin, tm=256, name=tag + "_outproj_p")

    ms = xs.shape[0]
    proj_s = norm_matmul(xs, norm_g, wts["w0"], tm=ms, tn=MLA_PROJ_W, out_dtype=BF16, name=tag + "_inproj_s")
    qs, ckv_s, kpe_s = mid(proj_s, *tabs_s, tm=ms, with_kv=False, name=tag + "_mid_s")
    os_ = sample_attention(qs, ckv_s, kpe_s, ckv_cache, kpe_cache, wts["wk"], wts["wvt"], name=tag + "_attn_s")
    xs = out_proj(os_, proj_s, wts["wo"], xs, fin, tm=ms, name=tag + "_outproj_s")
    return xp, xs, ckv_p, kpe_p, ckv_s, kpe_s


def _ssd_layer(xp, xs, l2_norm, l2_w_in, l2_conv_w, l2_conv_b, l2_dt_bias, l2_a_log, l2_d_skip, l2_gnorm, l2_w_out,
               state_l2_conv, state_l2_ssm, *, batch, seq, nb, new, chunk_p=256, chunk_s=None):
    d = xp.shape[1]
    ms = nb * new
    inner = SSM_GROUPS * SSM_GROUP_W
    bc = SSM_GROUPS * D_STATE
    main_w = 2 * inner + 2 * bc
    w2 = l2_w_in.astype(BF16)
    w2dt = jnp.pad(l2_w_in[:, main_w:].reshape(d, SSM_GROUPS, SSM_GROUP_HEADS),
                   ((0, 0), (0, 0), (0, LANE - SSM_GROUP_HEADS))).reshape(d, SSM_GROUPS * LANE).astype(BF16)
    w2o = l2_w_out.astype(BF16)
    dtal = jnp.pad(jnp.stack([l2_dt_bias, l2_a_log]).reshape(2, SSM_GROUPS, SSM_GROUP_HEADS).transpose(1, 0, 2),
                   ((0, 0), (0, 0), (0, LANE - SSM_GROUP_HEADS)))
    dskip = jnp.repeat(l2_d_skip, SSM_HEAD_DIM).reshape(1, inner)
    hist_rows = SSM_CONV_K - 1

    def run_ssd(proj, dtp, hist, st0, nbatch, nchunks, L, valid, name):
        hist = jnp.pad(hist, ((0, 0), (SSM_HIST - hist_rows, 0), (0, 0)))
        y, tx, tb, tc, st = ssd_mixer(
            proj, dtp, hist, st0.reshape(nbatch, inner, D_STATE), l2_conv_w, l2_conv_b.reshape(1, -1), dtal, dskip,
            l2_gnorm.reshape(1, inner), batch=nbatch, nchunks=nchunks, L=L, valid=valid, name=name)
        tail = jnp.concatenate([tx, tb, tc], axis=-1)[:, SSM_HIST - hist_rows:]
        return y, tail, st.reshape(nbatch, SSM_GROUPS * SSM_GROUP_HEADS, SSM_HEAD_DIM, D_STATE)

    proj_p = norm_matmul(xp, l2_norm, w2, tm=256, tn=main_w // 2, n_out=main_w, out_dtype=BF16, name="l2_inproj_p")
    dt_p = norm_matmul(xp, l2_norm, w2dt, tm=512, tn=SSM_GROUPS * LANE, out_dtype=F32, name="l2_dtproj_p")
    y_p, l2_conv_p, l2_ssm_p = run_ssd(
        proj_p, dt_p, jnp.zeros((batch, hist_rows, inner + 2 * bc), F32),
        jnp.zeros((batch, inner, D_STATE), F32), batch, seq // chunk_p, chunk_p, chunk_p, "l2_ssd_p")
    xp = out_proj(y_p, None, w2o, xp, None, tm=256, name="l2_outproj_p")

    proj_s = norm_matmul(xs, l2_norm, w2, tm=ms, tn=main_w // 2, n_out=main_w, out_dtype=BF16, name="l2_inproj_s")
    dt_s = norm_matmul(xs, l2_norm, w2dt, tm=ms, tn=SSM_GROUPS * LANE, out_dtype=F32, name="l2_dtproj_s")
    L = chunk_s if chunk_s is not None else -(-new // SUBLANES) * SUBLANES
    pad_rows = lambda a: jnp.pad(a.reshape(nb, new, a.shape[1]), ((0, 0), (0, L - new), (0, 0))).reshape(nb * L, a.shape[1])
    y_s, l2_conv_s, l2_ssm_s = run_ssd(pad_rows(proj_s), pad_rows(dt_s), state_l2_conv, state_l2_ssm,
                                       nb, 1, L, new, "l2_ssd_s")
    y_s = y_s.reshape(nb, L, inner)[:, :new].reshape(ms, inner)
    xs = out_proj(y_s, None, w2o, xs, None, tm=ms, name="l2_outproj_s")

    return xp, xs, l2_conv_p, l2_ssm_p, l2_conv_s, l2_ssm_s


def kernel(x_prompt, x_sample, cache_l0_ckv, cache_l0_kpe, state_l1_conv, state_l2_conv, state_l2_ssm, cache_l3_ckv, cache_l3_kpe, l0_norm, l0_w_in, l0_q_norm, l0_w_qb, l0_kv_norm, l0_w_kvb, l0_w_out, l1_norm, l1_w_in, l1_dw_w, l1_dw_b, l1_ln_g, l1_ln_b, l1_w_out, l2_norm, l2_w_in, l2_conv_w, l2_conv_b, l2_dt_bias, l2_a_log, l2_d_skip, l2_gnorm, l2_w_out, l3_norm, l3_w_in, l3_q_norm, l3_w_qb, l3_kv_norm, l3_w_kvb, l3_w_out, final_norm):
    batch, seq, d = x_prompt.shape
    nb, new, _ = x_sample.shape
    past = cache_l0_ckv.shape[1]
    xp = x_prompt.reshape(batch * seq, d)
    xs = x_sample.reshape(nb * new, d)
    ms = nb * new

    tabs_p = _rope_tables(np.arange(seq))
    cs, sn = _rope_tables(past + np.arange(new))
    tabs_s = (np.tile(cs, (nb, 1)), np.tile(sn, (nb, 1)))

    w0 = _mla_weights(l0_w_in, l0_w_qb, l0_w_kvb, l0_w_out)
    xp, xs, l0_ckv_p, l0_kpe_p, l0_ckv_s, l0_kpe_s = _mla_layer(
        xp, xs, l0_norm, l0_q_norm, l0_kv_norm, w0, cache_l0_ckv, cache_l0_kpe, tabs_p, tabs_s,
        batch=batch, seq=seq, fin=None, tag="l0")

    w1 = l1_w_in.astype(BF16)
    w1o = l1_w_out.astype(BF16)
    pad_hist = lambda h: jnp.pad(h, ((0, 0), (CONV_HIST - (CONV_K - 1), 0), (0, 0)))
    proj_p = norm_matmul(xp, l1_norm, w1, tm=512, tn=w1.shape[1] // 2, out_dtype=BF16, name="l1_inproj_p")
    ln1 = (l1_ln_g, l1_ln_b)
    gate_block = 2
    c_p, tail_p = conv_mixer(proj_p, jnp.zeros((batch, CONV_HIST, d), F32), l1_dw_w, l1_dw_b,
                             batch=batch, seq=seq, tt=128, name="l1_conv_p")
    xp = out_proj(c_p, proj_p, w1o, xp, None, tm=256, gate_block=gate_block, ln=ln1, name="l1_outproj_p")
    proj_s = norm_matmul(xs, l1_norm, w1, tm=ms, tn=w1.shape[1] // 2, out_dtype=BF16, name="l1_inproj_s")
    c_s, tail_s = conv_mixer(proj_s, pad_hist(state_l1_conv), l1_dw_w, l1_dw_b,
                             batch=nb, seq=new, tt=new, name="l1_conv_s")
    xs = out_proj(c_s, proj_s, w1o, xs, None, tm=ms, gate_block=gate_block, ln=ln1, name="l1_outproj_s")
    l1_conv_p = tail_p[:, CONV_HIST - (CONV_K - 1):]
    l1_conv_s = tail_s[:, CONV_HIST - (CONV_K - 1):]

    xp, xs, l2_conv_p, l2_ssm_p, l2_conv_s, l2_ssm_s = _ssd_layer(
        xp, xs, l2_norm, l2_w_in, l2_conv_w, l2_conv_b, l2_dt_bias, l2_a_log, l2_d_skip, l2_gnorm, l2_w_out,
        state_l2_conv, state_l2_ssm, batch=batch, seq=seq, nb=nb, new=new)

    w3 = _mla_weights(l3_w_in, l3_w_qb, l3_w_kvb, l3_w_out)
    yp, ys, l3_ckv_p, l3_kpe_p, l3_ckv_s, l3_kpe_s = _mla_layer(
        xp, xs, l3_norm, l3_q_norm, l3_kv_norm, w3, cache_l3_ckv, cache_l3_kpe, tabs_p, tabs_s,
        batch=batch, seq=seq, fin=final_norm, tag="l3")

    r3 = lambda a, n: a.reshape(n, -1, a.shape[-1])
    return (yp.reshape(batch, seq, d), ys.reshape(nb, new, d),
            r3(l0_ckv_p, batch), r3(l0_kpe_p, batch), r3(l0_ckv_s, nb), r3(l0_kpe_s, nb),
            l1_conv_p, l1_conv_s,
            l2_conv_p, l2_ssm_p, l2_conv_s, l2_ssm_s,
            r3(l3_ckv_p, batch), r3(l3_kpe_p, batch), r3(l3_ckv_s, nb), r3(l3_kpe_s, nb))
```
